```python
import math
import jax
import jax.numpy as jnp
from jax import lax
import numpy as np

D_MODEL = 1024
BATCH = 8
SEQ = 4096
DEPTH = 4

HEAD_DIM = 64
ROPE_THETA = 10000.0
GRID_W = 64
Q_BLOCK = 128
EPS = 1e-6
NEG_INF = -1e30

DA_HEADS = 4
GQ_HEADS = 8
GQ_KV = 2
DIL_GROUPS = ((128, 1), (512, 4), (2048, 16))
DIL_HEADS = 8
WIN_BLOCK = 64
NA_HEADS = 8
NA_ROWS = 8
NA_COLS = 16
NA_QC = 16
NA_KC_BLK = NA_QC + NA_COLS

N_BRANCH = 4
BRANCH_W = 512
D_FF = 2816
CONV_W = 3

A_QK = DA_HEADS * 2 * HEAD_DIM
A_V = DA_HEADS * 2 * HEAD_DIM
B_Q = GQ_HEADS * HEAD_DIM
B_KV = GQ_KV * HEAD_DIM
C_QKV = len(DIL_GROUPS) * DIL_HEADS * HEAD_DIM
D_QKV = NA_HEADS * HEAD_DIM
GATE_W = N_BRANCH * D_MODEL
PART_SIZES = (A_QK, A_QK, A_V, B_Q, B_KV, B_KV, C_QKV, C_QKV, C_QKV, D_QKV, D_QKV, D_QKV, GATE_W)
N_IN = sum(PART_SIZES)

kernel_name = 'hybrid_gated_mixer_encoder'


def rms_norm(x, g):
    xf = x.astype(jnp.float32)
    y = xf * lax.rsqrt(jnp.mean(xf * xf, axis=-1, keepdims=True) + EPS)
    return (y * g.astype(jnp.float32)).astype(x.dtype)


def rope_angles(pos, dim):
    inv = ROPE_THETA ** (-(jnp.arange(0, dim, 2, dtype=jnp.float32) / dim))
    return pos.astype(jnp.float32)[:, None] * inv[None, :]


def rope(x, ang):
    f = ang.shape[-1]
    shape = (1, ang.shape[0]) + (1,) * (x.ndim - 3) + (f,)
    cos = jnp.cos(ang).reshape(shape).astype(x.dtype)
    sin = jnp.sin(ang).reshape(shape).astype(x.dtype)
    x1, x2 = x[..., :f], x[..., f:]
    return jnp.concatenate([x1 * cos - x2 * sin, x2 * cos + x1 * sin], axis=-1)


def axial_rope(x, ang_row, ang_col):
    half = HEAD_DIM // 2
    return jnp.concatenate([rope(x[..., :half], ang_row), rope(x[..., half:], ang_col)], axis=-1)


def split_parts(p):
    out = []
    off = 0
    for n in PART_SIZES:
        out.append(p[..., off:off + n])
        off += n
    return out


def dense_block_sweep(fn, q):
    b, s = q.shape[:2]
    nb = s // Q_BLOCK
    qb = jnp.moveaxis(q.reshape((b, nb, Q_BLOCK) + q.shape[2:]), 1, 0)
    ob = lax.map(fn, qb)
    return jnp.moveaxis(ob, 0, 1).reshape((b, s) + ob.shape[3:])


def diff_attention(q, k, v, lam, subln_g, lambda_init):
    lf = lam.astype(jnp.float32)
    lam_val = jnp.exp(jnp.sum(lf[0] * lf[1])) - jnp.exp(jnp.sum(lf[2] * lf[3])) + lambda_init
    scale = HEAD_DIM ** -0.5

    def block(qb):
        sc = jnp.einsum('bqhcd,bkhcd->bhcqk', qb, k).astype(jnp.float32) * scale
        p = jax.nn.softmax(sc, axis=-1)
        a = p[:, :, 0] - lam_val * p[:, :, 1]
        return jnp.einsum('bhqk,bkhe->bqhe', a.astype(v.dtype), v)

    o = dense_block_sweep(block, q)
    o = rms_norm(o, subln_g) * (1.0 - lambda_init)
    return o.reshape(o.shape[0], o.shape[1], -1)


def gqa_attention(q, k, v):
    b, s, hq, dh = q.shape
    g = k.shape[2]
    q = q.reshape(b, s, g, hq // g, dh)
    scale = dh ** -0.5

    def block(qb):
        sc = jnp.einsum('bqgrd,bkgd->bgrqk', qb, k).astype(jnp.float32) * scale
        p = jax.nn.softmax(sc, axis=-1).astype(v.dtype)
        return jnp.einsum('bgrqk,bkgd->bqgrd', p, v)

    o = dense_block_sweep(block, q)
    return o.reshape(b, s, hq * dh)


def banded_window_attention(q, k, v, radius):
    n, L, h, dh = q.shape
    nb = -(-L // WIN_BLOCK)
    lp = nb * WIN_BLOCK
    span = WIN_BLOCK + 2 * radius
    qb = jnp.pad(q, ((0, 0), (0, lp - L), (0, 0), (0, 0))).reshape(n, nb, WIN_BLOCK, h, dh)
    pad_kv = ((0, 0), (radius, lp - L + radius), (0, 0), (0, 0))
    idx = np.arange(nb)[:, None] * WIN_BLOCK + np.arange(span)[None, :]
    kb = jnp.pad(k, pad_kv)[:, idx]
    vb = jnp.pad(v, pad_kv)[:, idx]
    key_pos = idx - radius
    q_pos = np.arange(nb)[:, None] * WIN_BLOCK + np.arange(WIN_BLOCK)[None, :]
    ok = ((np.abs(key_pos[:, None, :] - q_pos[:, :, None]) <= radius)
          & (key_pos[:, None, :] >= 0) & (key_pos[:, None, :] < L))
    sc = jnp.einsum('nbqhd,nbkhd->nbhqk', qb, kb).astype(jnp.float32) * (dh ** -0.5)
    sc = jnp.where(ok[None, :, None], sc, NEG_INF)
    lse = jax.nn.logsumexp(sc, axis=-1)
    p = jnp.exp(sc - lse[..., None]).astype(v.dtype)
    o = jnp.einsum('nbhqk,nbkhd->nbqhd', p, vb).reshape(n, lp, h, dh)[:, :L]
    lse = jnp.transpose(lse, (0, 1, 3, 2)).reshape(n, lp, h)[:, :L]
    return o, lse


def dilated_group(q, k, v, window, dilation):
    b, s, h, dh = q.shape
    L = s // dilation
    radius = window // (2 * dilation)

    def to_sub(t):
        return t.reshape(b, L, dilation, h, dh).transpose(0, 2, 1, 3, 4).reshape(b * dilation, L, h, dh)

    o, lse = banded_window_attention(to_sub(q), to_sub(k), to_sub(v), radius)
    o = o.reshape(b, dilation, L, h, dh).transpose(0, 2, 1, 3, 4).reshape(b, s, h, dh)
    lse = lse.reshape(b, dilation, L, h).transpose(0, 2, 1, 3).reshape(b, s, h)
    return o, lse


def dilated_attention(q, k, v):
    b, s = q.shape[:2]
    outs, lses = [], []
    for gi, (window, dilation) in enumerate(DIL_GROUPS):
        o, lse = dilated_group(q[:, :, gi], k[:, :, gi], v[:, :, gi], window, dilation)
        outs.append(o)
        lses.append(lse)
    w = jax.nn.softmax(jnp.stack(lses, axis=0), axis=0)
    o = jnp.einsum('gbsh,gbshd->bshd', w, jnp.stack(outs, axis=0).astype(jnp.float32))
    return o.astype(q.dtype).reshape(b, s, -1)


def neighbourhood_attention(q, k, v, rpb):
    b, s, h, dh = q.shape
    rows = s // GRID_W
    kr = min(NA_ROWS, rows)
    ncb = GRID_W // NA_QC
    q_cols = np.arange(ncb)[:, None] * NA_QC + np.arange(NA_QC)[None, :]
    blk_start = np.clip(np.arange(ncb) * NA_QC - NA_COLS // 2, 0, GRID_W - NA_KC_BLK)
    key_cols = blk_start[:, None] + np.arange(NA_KC_BLK)[None, :]
    win_start = np.clip(q_cols - NA_COLS // 2, 0, GRID_W - NA_COLS)
    col_ok = ((key_cols[:, None, :] >= win_start[:, :, None])
              & (key_cols[:, None, :] < win_start[:, :, None] + NA_COLS))
    col_idx = np.clip(key_cols[:, None, :] - q_cols[:, :, None] + NA_COLS - 1, 0, 2 * NA_COLS - 2)
    qg = q.reshape(b, rows, GRID_W, h, dh)
    kg = k.reshape(b, rows, GRID_W, h, dh)[:, :, key_cols]
    vg = v.reshape(b, rows, GRID_W, h, dh)[:, :, key_cols]
    rpb_f = rpb.astype(jnp.float32)
    scale = dh ** -0.5

    def row(i):
        rs = jnp.clip(i - kr // 2, 0, rows - kr)
        kb = lax.dynamic_slice_in_dim(kg, rs, kr, axis=1)
        vb = lax.dynamic_slice_in_dim(vg, rs, kr, axis=1)
        qi = lax.dynamic_index_in_dim(qg, i, axis=1, keepdims=False).reshape(b, ncb, NA_QC, h, dh)
        sc = jnp.einsum('bcqhd,brckhd->bhcqrk', qi, kb).astype(jnp.float32) * scale
        rel_r = rs + jnp.arange(kr) - i + NA_ROWS - 1
        bias = rpb_f[:, rel_r][:, :, col_idx]
        sc = sc + jnp.transpose(bias, (0, 2, 3, 1, 4))[None]
        sc = jnp.where(col_ok[None, None, :, :, None, :], sc, NEG_INF)
        p = jax.nn.softmax(sc.reshape(sc.shape[:4] + (kr * NA_KC_BLK,)), axis=-1).reshape(sc.shape)
        o = jnp.einsum('bhcqrk,brckhd->bcqhd', p.astype(v.dtype), vb)
        return o.reshape(b, GRID_W, h, dh)

    o = lax.map(row, jnp.arange(rows))
    return jnp.moveaxis(o, 0, 1).reshape(b, s, h * dh)


def conv_ffn(h, w_up, conv_w, conv_b, w_down):
    u = h @ w_up
    u = lax.conv_general_dilated(u, conv_w[:, None, :], window_strides=(1,), padding=((CONV_W // 2, CONV_W // 2),),
                                 dimension_numbers=('NWC', 'WIO', 'NWC'), feature_group_count=u.shape[-1]) + conv_b
    a, g = u[..., :D_FF], u[..., D_FF:]
    return (jax.nn.silu(a) * g) @ w_down


def setup_inputs(seed: int = 0) -> dict:
    key = jax.random.key(seed)
    ks = jax.random.split(key, 16)

    def nrm(k, shape, scale):
        return jax.random.normal(k, shape, jnp.float32) * scale

    return {
        'x': nrm(ks[0], (BATCH, SEQ, D_MODEL), 1.0),
        'norm1_g': 1.0 + nrm(ks[1], (DEPTH, D_MODEL), 0.02),
        'w_in': nrm(ks[2], (DEPTH, D_MODEL, N_IN), D_MODEL ** -0.5),
        'qk_g': 1.0 + nrm(ks[3], (DEPTH, N_BRANCH, 2, HEAD_DIM), 0.02),
        'lam': nrm(ks[4], (DEPTH, 4, HEAD_DIM), 0.1),
        'subln_g': 1.0 + nrm(ks[5], (DEPTH, 2 * HEAD_DIM), 0.02),
        'rpb': nrm(ks[6], (DEPTH, NA_HEADS, 2 * NA_ROWS - 1, 2 * NA_COLS - 1), 0.1),
        'w_branch': nrm(ks[7], (DEPTH, N_BRANCH, BRANCH_W, D_MODEL), BRANCH_W ** -0.5),
        'w_out': nrm(ks[8], (DEPTH, D_MODEL, D_MODEL), D_MODEL ** -0.5),
        'norm2_g': 1.0 + nrm(ks[9], (DEPTH, D_MODEL), 0.02),
        'w_up': nrm(ks[10], (DEPTH, D_MODEL, 2 * D_FF), D_MODEL ** -0.5),
        'conv_w': nrm(ks[11], (DEPTH, CONV_W, 2 * D_FF), CONV_W ** -0.5),
        'conv_b': nrm(ks[12], (DEPTH, 2 * D_FF), 0.02),
        'w_down': nrm(ks[13], (DEPTH, D_FF, D_MODEL), D_FF ** -0.5),
    }


def reference(x, norm1_g, w_in, qk_g, lam, subln_g, rpb, w_branch, w_out, norm2_g, w_up, conv_w, conv_b, w_down):
    b, s, _ = x.shape
    pos = jnp.arange(s, dtype=jnp.int32)
    ang_1d = rope_angles(pos, HEAD_DIM)
    ang_row = rope_angles(pos // GRID_W, HEAD_DIM // 2)
    ang_col = rope_angles(pos % GRID_W, HEAD_DIM // 2)
    n_dil = len(DIL_GROUPS)
    for l in range(DEPTH):
        lambda_init = 0.8 - 0.6 * math.exp(-0.3 * l)
        h = rms_norm(x, norm1_g[l])
        aq, ak, av, bq, bk, bv, cq, ck, cv, dq, dk, dv, g = split_parts(h @ w_in[l])
        gq = qk_g[l]
        aq = rope(rms_norm(aq.reshape(b, s, DA_HEADS, 2, HEAD_DIM), gq[0, 0]), ang_1d)
        ak = rope(rms_norm(ak.reshape(b, s, DA_HEADS, 2, HEAD_DIM), gq[0, 1]), ang_1d)
        o_a = diff_attention(aq, ak, av.reshape(b, s, DA_HEADS, 2 * HEAD_DIM), lam[l], subln_g[l], lambda_init)
        bq = axial_rope(rms_norm(bq.reshape(b, s, GQ_HEADS, HEAD_DIM), gq[1, 0]), ang_row, ang_col)
        bk = axial_rope(rms_norm(bk.reshape(b, s, GQ_KV, HEAD_DIM), gq[1, 1]), ang_row, ang_col)
        o_b = gqa_attention(bq, bk, bv.reshape(b, s, GQ_KV, HEAD_DIM))
        cq = rope(rms_norm(cq.reshape(b, s, n_dil, DIL_HEADS, HEAD_DIM), gq[2, 0]), ang_1d)
        ck = rope(rms_norm(ck.reshape(b, s, n_dil, DIL_HEADS, HEAD_DIM), gq[2, 1]), ang_1d)
        o_c = dilated_attention(cq, ck, cv.reshape(b, s, n_dil, DIL_HEADS, HEAD_DIM))
        dq = rms_norm(dq.reshape(b, s, NA_HEADS, HEAD_DIM), gq[3, 0])
        dk = rms_norm(dk.reshape(b, s, NA_HEADS, HEAD_DIM), gq[3, 1])
        o_d = neighbourhood_attention(dq, dk, dv.reshape(b, s, NA_HEADS, HEAD_DIM), rpb[l])
        gate = jax.nn.sigmoid(g.reshape(b, s, N_BRANCH, D_MODEL).astype(jnp.float32)).astype(x.dtype)
        merged = gate[:, :, 0] * (o_a @ w_branch[l, 0])
        for i, o_i in ((1, o_b), (2, o_c), (3, o_d)):
            merged = merged + gate[:, :, i] * (o_i @ w_branch[l, i])
        x = x + merged @ w_out[l]
        x = x + conv_ffn(rms_norm(x, norm2_g[l]), w_up[l], conv_w[l], conv_b[l], w_down[l])
    return x
```

```python
import functools
import math

import jax
import jax.numpy as jnp
import numpy as np
from jax import lax
from jax.experimental import pallas as pl
from jax.experimental.pallas import tpu as pltpu

D_MODEL = 1024
HEAD_DIM = 64
LANES = 128
MXU_COLS = 256
ROPE_THETA = 10000.0
GRID_W = 64
EPS = 1e-6
NEG_INF = -1e30
SCALE = HEAD_DIM ** -0.5

DA_HEADS = 4
GQ_HEADS = 8
GQ_KV = 2
DIL_GROUPS = ((128, 1), (512, 4), (2048, 16))
DIL_HEADS = 8
DIL_RADIUS = 64
NA_HEADS = 8
NA_ROWS = 8
NA_COLS = 16
N_BRANCH = 4
BRANCH_W = 512
D_FF = 2816
CONV_HALO = 16
TOKEN_TILE = 1024
ATTN_Q_TILE = 256
MERGE_TILE = 512
FFN_COL_TILE = 256

_OFF = {}
_o = 0
for _name, _n in (("aq", 512), ("ak", 512), ("av", 512), ("bq", 512), ("bk", 128), ("bv", 128),
                  ("cq", 1536), ("ck", 1536), ("cv", 1536), ("dq", 512), ("dk", 512), ("dv", 512),
                  ("g", 4096)):
    _OFF[_name] = (_o, _n)
    _o += _n

VMEM_LIMIT = 56 * 1024 * 1024

BF16 = jnp.bfloat16
F32 = jnp.float32


def _cparams(n_axes):
    return pltpu.CompilerParams(dimension_semantics=("arbitrary",) * n_axes,
                                vmem_limit_bytes=VMEM_LIMIT)


def _group_sumsq(y, bd):
    sq = y * y
    hi = sq.astype(BF16)
    lo = (sq - hi.astype(F32)).astype(BF16)
    return (jnp.dot(hi, bd, preferred_element_type=F32)
            + jnp.dot(lo, bd, preferred_element_type=F32))


def _proj_kernel(*refs, mode, tn, shift):
    if mode == "rope":
        x_ref, g1_ref, w_ref, gain_ref, bd_ref, c_ref, sa_ref, sb_ref, o_ref, h_scr = refs
    elif mode == "norm":
        x_ref, g1_ref, w_ref, gain_ref, bd_ref, o_ref, h_scr = refs
    else:
        x_ref, g1_ref, w_ref, o_ref, h_scr = refs

    @pl.when(pl.program_id(1) == 0)
    def _():
        x = x_ref[...]
        ms = jnp.mean(x * x, axis=-1, keepdims=True)
        h_scr[...] = (x * lax.rsqrt(ms + EPS) * g1_ref[...]).astype(BF16)

    y = jnp.dot(h_scr[...], w_ref[...], preferred_element_type=F32)
    if mode == "plain":
        o_ref[...] = y.astype(o_ref.dtype)
    elif mode == "sigmoid":
        o_ref[...] = (1.0 / (1.0 + jnp.exp(-y))).astype(o_ref.dtype)
    else:
        bd = bd_ref[...]
        parts = []
        for c in range(tn // MXU_COLS):
            yc = y[:, c * MXU_COLS:(c + 1) * MXU_COLS]
            ss = _group_sumsq(yc, bd)
            parts.append(yc * lax.rsqrt(ss * (1.0 / HEAD_DIM) + EPS))
        n = (parts[0] if len(parts) == 1 else jnp.concatenate(parts, axis=1)) * gain_ref[...]
        if mode == "rope":
            reps = tn // LANES
            cos = jnp.concatenate([c_ref[...]] * reps, axis=1)
            sa = jnp.concatenate([sa_ref[...]] * reps, axis=1)
            sb = jnp.concatenate([sb_ref[...]] * reps, axis=1)
            n = (n * cos + pltpu.roll(n, tn - shift, 1) * sa + pltpu.roll(n, shift, 1) * sb)
        o_ref[...] = n.astype(o_ref.dtype)


def _proj(x2, g1, w, *, mode, tm, tn, seq, gain=None, tables=None, shift=0):
    t, d = x2.shape
    n = w.shape[1]
    assert t % tm == 0 and n % tn == 0 and seq % tm == 0
    in_specs = [pl.BlockSpec((tm, d), lambda i, j: (i, 0)),
                pl.BlockSpec((1, d), lambda i, j: (0, 0)),
                pl.BlockSpec((d, tn), lambda i, j: (0, j))]
    args = [x2, g1, w]
    if mode in ("rope", "norm"):
        assert tn % MXU_COLS == 0
        blk = np.kron(np.eye(MXU_COLS // HEAD_DIM), np.ones((HEAD_DIM, HEAD_DIM)))
        in_specs += [pl.BlockSpec((1, tn), lambda i, j: (0, j)),
                     pl.BlockSpec((MXU_COLS, MXU_COLS), lambda i, j: (0, 0))]
        args += [gain, jnp.asarray(blk, BF16)]
    if mode == "rope":
        per_seq = seq // tm
        for tb in tables:
            in_specs.append(pl.BlockSpec((tm, LANES), lambda i, j: (i % per_seq, 0)))
            args.append(tb)
    return pl.pallas_call(
        functools.partial(_proj_kernel, mode=mode, tn=tn, shift=shift),
        out_shape=jax.ShapeDtypeStruct((t, n), BF16),
        grid=(t // tm, n // tn),
        in_specs=in_specs,
        out_specs=pl.BlockSpec((tm, tn), lambda i, j: (i, j)),
        scratch_shapes=[pltpu.VMEM((tm, d), BF16)],
        compiler_params=_cparams(2),
        name=f"proj_{mode}{shift}",
    )(*args)


def _half_masks(rows):
    lane = lax.broadcasted_iota(jnp.int32, (rows, LANES), 1)
    return lane < HEAD_DIM


def _masked_q(q_ref_val, lo_mask):
    qf = q_ref_val.astype(F32) * SCALE
    q_lo = jnp.where(lo_mask, qf, 0.0).astype(BF16)
    q_hi = jnp.where(lo_mask, 0.0, qf).astype(BF16)
    return q_lo, q_hi


def _scores(q, k):
    return lax.dot_general(q, k, (((1,), (1,)), ((), ())), preferred_element_type=F32)


def _diff_attn_kernel(q_ref, k_ref, v_ref, lam_ref, sg_ref, o_ref, *, lambda_init):
    tq = q_ref.shape[0]
    lo = _half_masks(tq)
    q1, q2 = _masked_q(q_ref[...], lo)
    k = k_ref[...]
    lam = lam_ref[...]
    lam_val = (jnp.exp(jnp.sum(lam[0:1] * lam[1:2], axis=-1, keepdims=True))
               - jnp.exp(jnp.sum(lam[2:3] * lam[3:4], axis=-1, keepdims=True)) + lambda_init)
    s1 = _scores(q1, k)
    e1 = jnp.exp(s1 - jnp.max(s1, axis=-1, keepdims=True))
    r1 = 1.0 / jnp.sum(e1, axis=-1, keepdims=True)
    s2 = _scores(q2, k)
    e2 = jnp.exp(s2 - jnp.max(s2, axis=-1, keepdims=True))
    r2 = lam_val / jnp.sum(e2, axis=-1, keepdims=True)
    a = (e1 * r1 - e2 * r2).astype(BF16)
    o = jnp.dot(a, v_ref[...], preferred_element_type=F32)
    ms = jnp.mean(o * o, axis=-1, keepdims=True)
    o_ref[...] = (o * lax.rsqrt(ms + EPS) * sg_ref[...] * (1.0 - lambda_init)).astype(o_ref.dtype)


def _diff_attention(r1, pv, lam, subln_g, lambda_init, *, tq):
    b, s, _ = r1.shape
    return pl.pallas_call(
        functools.partial(_diff_attn_kernel, lambda_init=lambda_init),
        out_shape=jax.ShapeDtypeStruct((b, s, BRANCH_W), BF16),
        grid=(b, DA_HEADS, s // tq),
        in_specs=[pl.BlockSpec((None, tq, LANES), lambda bi, h, i: (bi, i, h)),
                  pl.BlockSpec((None, s, LANES), lambda bi, h, i: (bi, 0, 4 + h)),
                  pl.BlockSpec((None, s, LANES), lambda bi, h, i: (bi, 0, 4 + h)),
                  pl.BlockSpec((4, HEAD_DIM), lambda bi, h, i: (0, 0)),
                  pl.BlockSpec((1, LANES), lambda bi, h, i: (0, 0))],
        out_specs=pl.BlockSpec((None, tq, LANES), lambda bi, h, i: (bi, i, h)),
        compiler_params=_cparams(3),
        name="diff_attn",
    )(r1, r1, pv, lam, subln_g)


def _gqa_kernel(q_ref, k_ref, v_ref, o_ref):
    tq = q_ref.shape[0]
    lo = _half_masks(tq)
    k = k_ref[...]
    v = v_ref[...]
    for c in range(q_ref.shape[1] // LANES):
        outs = []
        for qh in _masked_q(q_ref[:, c * LANES:(c + 1) * LANES], lo):
            s = _scores(qh, k)
            e = jnp.exp(s - jnp.max(s, axis=-1, keepdims=True))
            r = 1.0 / jnp.sum(e, axis=-1, keepdims=True)
            outs.append(jnp.dot(e.astype(BF16), v, preferred_element_type=F32) * r)
        o_ref[:, c * LANES:(c + 1) * LANES] = jnp.where(lo, outs[0], outs[1]).astype(o_ref.dtype)


def _gqa_attention(r2, pv, *, tq):
    b, s, _ = r2.shape
    qw = GQ_HEADS // GQ_KV * HEAD_DIM
    return pl.pallas_call(
        _gqa_kernel,
        out_shape=jax.ShapeDtypeStruct((b, s, BRANCH_W), BF16),
        grid=(b, GQ_KV, s // tq),
        in_specs=[pl.BlockSpec((None, tq, qw), lambda bi, g, i: (bi, i, g)),
                  pl.BlockSpec((None, s, LANES), lambda bi, g, i: (bi, 0, 4 + g)),
                  pl.BlockSpec((None, s, LANES), lambda bi, g, i: (bi, 0, 20 + g))],
        out_specs=pl.BlockSpec((None, tq, qw), lambda bi, g, i: (bi, i, g)),
        compiler_params=_cparams(3),
        name="gqa_attn",
    )(r2, r2, pv)


def _band_kernel(q_ref, k_ref, v_ref, o_ref, lse_ref, *, tq, span):
    sub_len = q_ref.shape[0]
    lo = _half_masks(tq)
    q_pos = lax.broadcasted_iota(jnp.int32, (tq, span), 0)
    k_pos = lax.broadcasted_iota(jnp.int32, (tq, span), 1)

    def block(ib, carry):
        i0 = pl.multiple_of(ib * tq, tq)
        k0 = pl.multiple_of(jnp.clip(i0 - DIL_RADIUS, 0, sub_len - span), DIL_RADIUS)
        k = k_ref[pl.ds(k0, span), :]
        v = v_ref[pl.ds(k0, span), :]
        ok = jnp.abs((k_pos + k0) - (q_pos + i0)) <= DIL_RADIUS
        outs, lses = [], []
        for qh in _masked_q(q_ref[pl.ds(i0, tq), :], lo):
            s = jnp.where(ok, _scores(qh, k), NEG_INF)
            m = jnp.max(s, axis=-1, keepdims=True)
            e = jnp.exp(s - m)
            l = jnp.sum(e, axis=-1, keepdims=True)
            outs.append(jnp.dot(e.astype(BF16), v, preferred_element_type=F32) * (1.0 / l))
            lses.append(jnp.broadcast_to(m + jnp.log(l), (tq, LANES)))
        o_ref[pl.ds(i0, tq), :] = jnp.where(lo, outs[0], outs[1]).astype(o_ref.dtype)
        lse_ref[pl.ds(i0, tq), :] = jnp.where(lo, lses[0], lses[1])
        return carry

    lax.fori_loop(0, sub_len // tq, block, 0)


def _dilated_group(r1, pv, gi, dilation):
    b, s, c1 = r1.shape
    cv = pv.shape[2]
    sub_len = s // dilation
    tq = min(128, sub_len)
    span = min(tq + 2 * DIL_RADIUS, sub_len)
    nb1, nbv, nbo = c1 // LANES, cv // LANES, BRANCH_W // LANES
    q_blk, k_blk, v_blk = 8 + 4 * gi, 20 + 4 * gi, 8 + 4 * gi
    r1v = r1.reshape(b, sub_len, dilation * c1)
    pvv = pv.reshape(b, sub_len, dilation * cv)
    o, lse = pl.pallas_call(
        functools.partial(_band_kernel, tq=tq, span=span),
        out_shape=(jax.ShapeDtypeStruct((b, sub_len, dilation * BRANCH_W), BF16),
                   jax.ShapeDtypeStruct((b, sub_len, dilation * BRANCH_W), F32)),
        grid=(b, dilation, 4),
        in_specs=[pl.BlockSpec((None, sub_len, LANES), lambda bi, r, hp: (bi, 0, r * nb1 + q_blk + hp)),
                  pl.BlockSpec((None, sub_len, LANES), lambda bi, r, hp: (bi, 0, r * nb1 + k_blk + hp)),
                  pl.BlockSpec((None, sub_len, LANES), lambda bi, r, hp: (bi, 0, r * nbv + v_blk + hp))],
        out_specs=(pl.BlockSpec((None, sub_len, LANES), lambda bi, r, hp: (bi, 0, r * nbo + hp)),
                   pl.BlockSpec((None, sub_len, LANES), lambda bi, r, hp: (bi, 0, r * nbo + hp))),
        compiler_params=_cparams(3),
        name=f"band_attn_d{dilation}",
    )(r1v, r1v, pvv)
    return o.reshape(b, s, BRANCH_W), lse.reshape(b, s, BRANCH_W)


def _nbr_kernel(q_ref, k_ref, v_ref, bias_ref, o_ref, *, rows):
    i = pl.program_id(1)
    kr = min(NA_ROWS, rows)
    rs = jnp.clip(i - kr // 2, 0, rows - kr)
    t0 = pl.multiple_of(rs * GRID_W, GRID_W)
    nk = kr * GRID_W
    lo = _half_masks(GRID_W)
    for hp in range(NA_HEADS // 2):
        k = k_ref[pl.ds(t0, nk), hp * LANES:(hp + 1) * LANES]
        v = v_ref[pl.ds(t0, nk), hp * LANES:(hp + 1) * LANES]
        outs = []
        for half, qh in enumerate(_masked_q(q_ref[:, hp * LANES:(hp + 1) * LANES], lo)):
            s = _scores(qh, k) + bias_ref[2 * hp + half]
            e = jnp.exp(s - jnp.max(s, axis=-1, keepdims=True))
            r = 1.0 / jnp.sum(e, axis=-1, keepdims=True)
            outs.append(jnp.dot(e.astype(BF16), v, preferred_element_type=F32) * r)
        o_ref[:, hp * LANES:(hp + 1) * LANES] = jnp.where(lo, outs[0], outs[1]).astype(o_ref.dtype)


def _nbr_bias_table(rpb, rows):
    kr = min(NA_ROWS, rows)
    c = np.arange(GRID_W)[:, None]
    kc = np.arange(GRID_W)[None, :]
    win = np.clip(c - NA_COLS // 2, 0, GRID_W - NA_COLS)
    col_ok = (kc >= win) & (kc < win + NA_COLS)
    col_idx = np.clip(kc - c + NA_COLS - 1, 0, 2 * NA_COLS - 2)
    n_off = NA_ROWS
    rel_r = np.clip(np.arange(n_off)[:, None] + np.arange(kr)[None, :], 0, 2 * NA_ROWS - 2)
    t = rpb.astype(F32)[:, rel_r][:, :, :, col_idx]
    t = jnp.where(col_ok[None, None, None], t, NEG_INF)
    t = jnp.transpose(t, (1, 0, 3, 2, 4))
    return t.reshape(n_off, NA_HEADS, GRID_W, kr * GRID_W)


def _nbr_attention(r3, pv, bias_tbl):
    b, s, _ = r3.shape
    rows = s // GRID_W
    kr = min(NA_ROWS, rows)
    nk = kr * GRID_W

    def off_of(i):
        return jnp.clip(i - kr // 2, 0, rows - kr) - i + NA_ROWS - 1

    return pl.pallas_call(
        functools.partial(_nbr_kernel, rows=rows),
        out_shape=jax.ShapeDtypeStruct((b, s, BRANCH_W), BF16),
        grid=(b, rows),
        in_specs=[pl.BlockSpec((None, GRID_W, BRANCH_W), lambda bi, i: (bi, i, 0)),
                  pl.BlockSpec((None, s, BRANCH_W), lambda bi, i: (bi, 0, 1)),
                  pl.BlockSpec((None, s, BRANCH_W), lambda bi, i: (bi, 0, 0)),
                  pl.BlockSpec((None, NA_HEADS, GRID_W, nk), lambda bi, i: (off_of(i), 0, 0, 0))],
        out_specs=pl.BlockSpec((None, GRID_W, BRANCH_W), lambda bi, i: (bi, i, 0)),
        compiler_params=_cparams(2),
        name="nbr_attn",
    )(r3, r3, pv, bias_tbl)


def _merge_kernel(x_ref, oa_ref, ob_ref, oc0_ref, oc1_ref, oc2_ref, l0_ref, l1_ref, l2_ref, od_ref,
                  gate_ref, wb_ref, wo_ref, o_ref):
    l0, l1, l2 = l0_ref[...], l1_ref[...], l2_ref[...]
    m = jnp.maximum(jnp.maximum(l0, l1), l2)
    w0, w1, w2 = jnp.exp(l0 - m), jnp.exp(l1 - m), jnp.exp(l2 - m)
    oc = (w0 * oc0_ref[...].astype(F32) + w1 * oc1_ref[...].astype(F32)
          + w2 * oc2_ref[...].astype(F32)) * (1.0 / (w0 + w1 + w2))
    branches = (oa_ref[...], ob_ref[...], oc.astype(BF16), od_ref[...])
    merged = None
    for bi, ob in enumerate(branches):
        t = (jnp.dot(ob, wb_ref[bi], preferred_element_type=F32)
             * gate_ref[:, bi * D_MODEL:(bi + 1) * D_MODEL].astype(F32))
        merged = t if merged is None else merged + t
    o_ref[...] = x_ref[...] + jnp.dot(merged.astype(BF16), wo_ref[...], preferred_element_type=F32)


def _merge(x2, oa, ob, ocs, lses, od, gate, wb, wo, *, tm):
    t, d = x2.shape
    row = lambda w: pl.BlockSpec((tm, w), lambda i: (i, 0))
    return pl.pallas_call(
        _merge_kernel,
        out_shape=jax.ShapeDtypeStruct((t, d), F32),
        grid=(t // tm,),
        in_specs=[row(d)] + [row(BRANCH_W)] * 9 + [row(N_BRANCH * d),
                  pl.BlockSpec((N_BRANCH, BRANCH_W, d), lambda i: (0, 0, 0)),
                  pl.BlockSpec((d, d), lambda i: (0, 0))],
        out_specs=row(d),
        compiler_params=_cparams(1),
        name="merge",
    )(x2, oa, ob, *ocs, *lses, od, gate, wb, wo)


def _ffn_kernel(x_ref, xp_ref, xn_ref, g2_ref, wa_ref, wg_ref, cwa_ref, cwg_ref, cba_ref, cbg_ref,
                wd_ref, o_ref, h_scr, acc_scr, *, tiles_per_seq):
    i, j = pl.program_id(0), pl.program_id(1)
    tm = x_ref.shape[0]
    ext = tm + 2 * CONV_HALO

    @pl.when(j == 0)
    def _():
        g2 = g2_ref[...]

        def norm(x):
            return x * lax.rsqrt(jnp.mean(x * x, axis=-1, keepdims=True) + EPS) * g2

        first = (i % tiles_per_seq) == 0
        last = (i % tiles_per_seq) == tiles_per_seq - 1
        h_scr[0:CONV_HALO, :] = jnp.where(first, 0.0, norm(xp_ref[...])).astype(BF16)
        h_scr[CONV_HALO:CONV_HALO + tm, :] = norm(x_ref[...]).astype(BF16)
        h_scr[CONV_HALO + tm:ext, :] = jnp.where(last, 0.0, norm(xn_ref[...])).astype(BF16)
        acc_scr[...] = jnp.zeros_like(acc_scr)

    h = h_scr[...]

    def conv(w_ref, cw_ref, cb_ref):
        u = jnp.dot(h, w_ref[...], preferred_element_type=F32)
        prev = pltpu.roll(u, 1, 0)[CONV_HALO:CONV_HALO + tm]
        nxt = pltpu.roll(u, ext - 1, 0)[CONV_HALO:CONV_HALO + tm]
        cw = cw_ref[...]
        return (cw[0:1] * prev + cw[1:2] * u[CONV_HALO:CONV_HALO + tm] + cw[2:3] * nxt
                + cb_ref[...])

    a = conv(wa_ref, cwa_ref, cba_ref)
    g = conv(wg_ref, cwg_ref, cbg_ref)
    hm = (a * (1.0 / (1.0 + jnp.exp(-a))) * g).astype(BF16)
    acc_scr[...] += jnp.dot(hm, wd_ref[...], preferred_element_type=F32)

    @pl.when(j == pl.num_programs(1) - 1)
    def _():
        o_ref[...] = x_ref[...] + acc_scr[...]


def _ffn(x2, g2, w_up, conv_w, conv_b, w_down, *, tm, tn, seq):
    t, d = x2.shape
    nj = D_FF // tn
    assert D_FF % tn == 0 and seq % tm == 0 and tm % CONV_HALO == 0
    hb = tm // CONV_HALO
    n_halo = t // CONV_HALO
    return pl.pallas_call(
        functools.partial(_ffn_kernel, tiles_per_seq=seq // tm),
        out_shape=jax.ShapeDtypeStruct((t, d), F32),
        grid=(t // tm, nj),
        in_specs=[pl.BlockSpec((tm, d), lambda i, j: (i, 0)),
                  pl.BlockSpec((CONV_HALO, d), lambda i, j: (jnp.maximum(i * hb - 1, 0), 0)),
                  pl.BlockSpec((CONV_HALO, d), lambda i, j: (jnp.minimum((i + 1) * hb, n_halo - 1), 0)),
                  pl.BlockSpec((1, d), lambda i, j: (0, 0)),
                  pl.BlockSpec((d, tn), lambda i, j: (0, j)),
                  pl.BlockSpec((d, tn), lambda i, j: (0, nj + j)),
                  pl.BlockSpec((3, tn), lambda i, j: (0, j)),
                  pl.BlockSpec((3, tn), lambda i, j: (0, nj + j)),
                  pl.BlockSpec((1, tn), lambda i, j: (0, j)),
                  pl.BlockSpec((1, tn), lambda i, j: (0, nj + j)),
                  pl.BlockSpec((tn, d), lambda i, j: (j, 0))],
        out_specs=pl.BlockSpec((tm, d), lambda i, j: (i, 0)),
        scratch_shapes=[pltpu.VMEM((tm + 2 * CONV_HALO, d), BF16), pltpu.VMEM((tm, d), F32)],
        compiler_params=_cparams(2),
        name="conv_ffn",
    )(x2, x2, x2, g2, w_up, w_up, conv_w, conv_w, conv_b, conv_b, w_down)


def _rope_tables(seq):
    pos = jnp.arange(seq, dtype=jnp.int32)

    def angles(p, dim):
        inv = ROPE_THETA ** (-(jnp.arange(0, dim, 2, dtype=F32) / dim))
        return p.astype(F32)[:, None] * inv[None, :]

    def tables(ang_per_lane, first_half):
        c, s = jnp.cos(ang_per_lane), jnp.sin(ang_per_lane)
        sa = jnp.where(first_half[None, :], -s, 0.0)
        sb = jnp.where(first_half[None, :], 0.0, s)
        return tuple(jnp.tile(t, (1, LANES // HEAD_DIM)) for t in (c, sa, sb))

    lane = np.arange(HEAD_DIM)
    half = HEAD_DIM // 2
    a1 = angles(pos, HEAD_DIM)
    t1 = tables(jnp.concatenate([a1, a1], axis=1), lane < half)
    ar = angles(pos // GRID_W, half)
    ac = angles(pos % GRID_W, half)
    t2 = tables(jnp.concatenate([ar, ar, ac, ac], axis=1), (lane % half) < half // 2)
    return t1, t2


def _cols(w, name):
    o, n = _OFF[name]
    return w[:, o:o + n]


def _dup_heads(w):
    k0, k1 = w[:, :HEAD_DIM], w[:, HEAD_DIM:]
    return jnp.concatenate([k0, k0, k1, k1], axis=1)


def _head_gain(g, n_heads):
    return jnp.tile(g.astype(F32), n_heads)


def kernel(x, norm1_g, w_in, qk_g, lam, subln_g, rpb, w_branch, w_out, norm2_g, w_up, conv_w, conv_b, w_down):
    b, s, d = x.shape
    depth = w_in.shape[0]
    t = b * s
    rows = s // GRID_W
    tm = min(TOKEN_TILE, s)
    t1d, tax = _rope_tables(s)
    x2 = x.reshape(t, d)

    for l in range(depth):
        lambda_init = 0.8 - 0.6 * math.exp(-0.3 * l)
        w = w_in[l]
        w_r1 = jnp.concatenate([_cols(w, n) for n in ("aq", "ak", "cq", "ck")], axis=1).astype(BF16)
        w_r2 = jnp.concatenate([_cols(w, "bq"), _dup_heads(_cols(w, "bk"))], axis=1).astype(BF16)
        w_r3 = jnp.concatenate([_cols(w, "dq"), _cols(w, "dk")], axis=1).astype(BF16)
        w_pv = jnp.concatenate([_cols(w, "dv"), _cols(w, "av"), _cols(w, "cv"),
                                _dup_heads(_cols(w, "bv"))], axis=1).astype(BF16)
        w_g = _cols(w, "g").astype(BF16)
        gq = qk_g[l]
        gain1 = jnp.concatenate([_head_gain(gq[0, 0], 8), _head_gain(gq[0, 1], 8),
                                 _head_gain(gq[2, 0], 24), _head_gain(gq[2, 1], 24)])[None, :]
        gain2 = jnp.concatenate([_head_gain(gq[1, 0], 8), _head_gain(gq[1, 1], 4)])[None, :]
        gain3 = jnp.concatenate([_head_gain(gq[3, 0], 8), _head_gain(gq[3, 1], 8)])[None, :]
        g1 = norm1_g[l].astype(F32)[None, :]

        r1 = _proj(x2, g1, w_r1, mode="rope", tm=tm, tn=512, seq=s, gain=gain1, tables=t1d,
                   shift=HEAD_DIM // 2).reshape(b, s, -1)
        r2 = _proj(x2, g1, w_r2, mode="rope", tm=tm, tn=768, seq=s, gain=gain2, tables=tax,
                   shift=HEAD_DIM // 4).reshape(b, s, -1)
        r3 = _proj(x2, g1, w_r3, mode="norm", tm=tm, tn=512, seq=s, gain=gain3).reshape(b, s, -1)
        pv = _proj(x2, g1, w_pv, mode="plain", tm=tm, tn=1408, seq=s).reshape(b, s, -1)
        gate = _proj(x2, g1, w_g, mode="sigmoid", tm=tm, tn=512, seq=s)

        o_a = _diff_attention(r1, pv, lam[l].astype(F32), subln_g[l].astype(F32)[None, :],
                              lambda_init, tq=min(ATTN_Q_TILE, s))
        o_b = _gqa_attention(r2, pv, tq=min(ATTN_Q_TILE, s))
        ocs, lses = [], []
        for gi, (_, dilation) in enumerate(DIL_GROUPS):
            o_g, lse_g = _dilated_group(r1, pv, gi, dilation)
            ocs.append(o_g.reshape(t, BRANCH_W))
            lses.append(lse_g.reshape(t, BRANCH_W))
        o_d = _nbr_attention(r3, pv, _nbr_bias_table(rpb[l], rows))

        x2 = _merge(x2, o_a.reshape(t, BRANCH_W), o_b.reshape(t, BRANCH_W), ocs, lses,
                    o_d.reshape(t, BRANCH_W), gate, w_branch[l].astype(BF16), w_out[l].astype(BF16),
                    tm=min(MERGE_TILE, s))
        x2 = _ffn(x2, norm2_g[l].astype(F32)[None, :], w_up[l].astype(BF16), conv_w[l].astype(F32),
                  conv_b[l].astype(F32)[None, :], w_down[l].astype(BF16), tm=tm, tn=FFN_COL_TILE, seq=s)
    return x2.reshape(b, s, d)
```

```python
import functools
import math

import jax
import jax.numpy as jnp
import numpy as np
from jax import lax
from jax.experimental import pallas as pl
from jax.experimental.pallas import tpu as pltpu

D_MODEL = 1024
HEAD_DIM = 64
LANES = 128
MXU_COLS = 256
ROPE_THETA = 10000.0
GRID_W = 64
EPS = 1e-6
NEG_INF = -1e30
SCALE = HEAD_DIM ** -0.5

DA_HEADS = 4
GQ_HEADS = 8
GQ_KV = 2
DIL_GROUPS = ((128, 1), (512, 4), (2048, 16))
DIL_HEADS = 8
DIL_RADIUS = 64
NA_HEADS = 8
NA_ROWS = 8
NA_COLS = 16
N_BRANCH = 4
BRANCH_W = 512
D_FF = 2816
CONV_HALO = 16
TOKEN_TILE = 1024
ATTN_Q_TILE = 256
MERGE_TILE = 512
FFN_COL_TILE = 256
BAND_SEGMENT = 1024
NBR_QROWS = 4
NBR_KROWS = NBR_QROWS + NA_ROWS

_OFF = {}
_o = 0
for _name, _n in (("aq", 512), ("ak", 512), ("av", 512), ("bq", 512), ("bk", 128), ("bv", 128),
                  ("cq", 1536), ("ck", 1536), ("cv", 1536), ("dq", 512), ("dk", 512), ("dv", 512),
                  ("g", 4096)):
    _OFF[_name] = (_o, _n)
    _o += _n

VMEM_LIMIT = 56 * 1024 * 1024

BF16 = jnp.bfloat16
F32 = jnp.float32


def _cparams(n_axes):
    return pltpu.CompilerParams(dimension_semantics=("arbitrary",) * n_axes,
                                vmem_limit_bytes=VMEM_LIMIT)


def _group_sumsq(y, bd):
    sq = y * y
    hi = sq.astype(BF16)
    lo = (sq - hi.astype(F32)).astype(BF16)
    return (jnp.dot(hi, bd, preferred_element_type=F32)
            + jnp.dot(lo, bd, preferred_element_type=F32))


def _proj_kernel(*refs, mode, tn, shift, dil, rope_blocks):
    xs_scr = None
    if dil > 1:
        refs, xs_scr = refs[:-1], refs[-1]
    if mode == "rope":
        x_ref, g1_ref, w_ref, gain_ref, bd_ref, c_ref, sa_ref, sb_ref, o_ref, h_scr = refs
    elif mode == "norm":
        x_ref, g1_ref, w_ref, gain_ref, bd_ref, o_ref, h_scr = refs
    else:
        x_ref, g1_ref, w_ref, o_ref, h_scr = refs
    j = pl.program_id(1)
    sub = x_ref.shape[0] // dil
    slabs = x_ref.shape[1] // LANES

    @pl.when(j == 0)
    def _():
        g1 = g1_ref[...]
        if dil > 1:
            for c in range(slabs):
                xs_scr[c] = x_ref[:, c * LANES:(c + 1) * LANES]
        for rho in range(dil):
            if dil == 1:
                x = x_ref[...]
            else:
                x = jnp.concatenate([xs_scr[c, pl.ds(rho, sub, stride=dil), :] for c in range(slabs)], axis=1)
            ms = jnp.mean(x * x, axis=-1, keepdims=True)
            h_scr[rho * sub:(rho + 1) * sub, :] = (x * lax.rsqrt(ms + EPS) * g1).astype(BF16)

    y = jnp.dot(h_scr[...], w_ref[...], preferred_element_type=F32)

    def store(val):
        val = val.astype(o_ref.dtype)
        if len(o_ref.shape) == 2:
            o_ref[...] = val
        else:
            for rho in range(dil):
                o_ref[rho] = val[rho * sub:(rho + 1) * sub]

    def normed():
        bd = bd_ref[...]
        parts = []
        for c in range(tn // MXU_COLS):
            yc = y[:, c * MXU_COLS:(c + 1) * MXU_COLS]
            ss = _group_sumsq(yc, bd)
            parts.append(yc * lax.rsqrt(ss * (1.0 / HEAD_DIM) + EPS))
        n = (parts[0] if len(parts) == 1 else jnp.concatenate(parts, axis=1)) * gain_ref[...]
        if mode == "rope":
            reps = tn // LANES
            cos = jnp.concatenate([c_ref[...]] * reps, axis=1)
            sa = jnp.concatenate([sa_ref[...]] * reps, axis=1)
            sb = jnp.concatenate([sb_ref[...]] * reps, axis=1)
            n = (n * cos + pltpu.roll(n, tn - shift, 1) * sa + pltpu.roll(n, shift, 1) * sb)
        store(n)

    if mode == "plain":
        store(y)
    elif mode == "sigmoid":
        store(1.0 / (1.0 + jnp.exp(-y)))
    elif rope_blocks is None:
        normed()
    else:
        pl.when(j < rope_blocks)(normed)
        pl.when(j >= rope_blocks)(lambda: store(y))


def _proj(x2, g1, w, *, mode, tm, tn, seq, gain=None, tables=None, shift=0, dil=None, rope_blocks=None):
    t, d = x2.shape
    n = w.shape[1]
    per_seq = seq // tm
    assert t % tm == 0 and n % tn == 0 and seq % tm == 0
    in_specs = [pl.BlockSpec((tm, d), lambda i, j: (i, 0)),
                pl.BlockSpec((1, d), lambda i, j: (0, 0)),
                pl.BlockSpec((d, tn), lambda i, j: (0, j))]
    args = [x2, g1, w]
    if mode in ("rope", "norm"):
        assert tn % MXU_COLS == 0
        blk = np.kron(np.eye(MXU_COLS // HEAD_DIM), np.ones((HEAD_DIM, HEAD_DIM)))
        in_specs += [pl.BlockSpec((1, tn), lambda i, j: (0, j)),
                     pl.BlockSpec((MXU_COLS, MXU_COLS), lambda i, j: (0, 0))]
        args += [gain, jnp.asarray(blk, BF16)]
    if mode == "rope":
        for tb in tables:
            in_specs.append(pl.BlockSpec((tm, LANES), lambda i, j: (i % per_seq, 0)))
            args.append(tb)
    if dil is None:
        out_shape = jax.ShapeDtypeStruct((t, n), BF16)
        out_spec = pl.BlockSpec((tm, tn), lambda i, j: (i, j))
    else:
        assert tm % (dil * 16) == 0
        out_shape = jax.ShapeDtypeStruct((t // seq, dil, seq // dil, n), BF16)
        out_spec = pl.BlockSpec((None, dil, tm // dil, tn), lambda i, j: (i // per_seq, 0, i % per_seq, j))
    scratch = [pltpu.VMEM((tm, d), BF16)]
    if dil is not None and dil > 1:
        scratch.append(pltpu.VMEM((d // LANES, tm, LANES), F32))
    return pl.pallas_call(
        functools.partial(_proj_kernel, mode=mode, tn=tn, shift=shift, dil=dil or 1, rope_blocks=rope_blocks),
        out_shape=out_shape,
        grid=(t // tm, n // tn),
        in_specs=in_specs,
        out_specs=out_spec,
        scratch_shapes=scratch,
        compiler_params=_cparams(2),
        name=f"proj_{mode}{shift}" + (f"_d{dil}" if dil else ""),
    )(*args)


def _half_masks(rows):
    lane = lax.broadcasted_iota(jnp.int32, (rows, LANES), 1)
    return lane < HEAD_DIM


def _masked_q(q_ref_val, lo_mask):
    qf = q_ref_val.astype(F32) * SCALE
    q_lo = jnp.where(lo_mask, qf, 0.0).astype(BF16)
    q_hi = jnp.where(lo_mask, 0.0, qf).astype(BF16)
    return q_lo, q_hi


def _scores(q, k):
    return lax.dot_general(q, k, (((1,), (1,)), ((), ())), preferred_element_type=F32)


def _diff_attn_kernel(q_ref, k_ref, v_ref, lam_ref, sg_ref, o_ref, *, lambda_init):
    tq = q_ref.shape[0]
    lo = _half_masks(tq)
    q1, q2 = _masked_q(q_ref[...], lo)
    k = k_ref[...]
    lam = lam_ref[...]
    lam_val = (jnp.exp(jnp.sum(lam[0:1] * lam[1:2], axis=-1, keepdims=True))
               - jnp.exp(jnp.sum(lam[2:3] * lam[3:4], axis=-1, keepdims=True)) + lambda_init)
    s1 = _scores(q1, k)
    e1 = jnp.exp(s1 - jnp.max(s1, axis=-1, keepdims=True))
    r1 = 1.0 / jnp.sum(e1, axis=-1, keepdims=True)
    s2 = _scores(q2, k)
    e2 = jnp.exp(s2 - jnp.max(s2, axis=-1, keepdims=True))
    r2 = lam_val / jnp.sum(e2, axis=-1, keepdims=True)
    a = (e1 * r1 - e2 * r2).astype(BF16)
    o = jnp.dot(a, v_ref[...], preferred_element_type=F32)
    ms = jnp.mean(o * o, axis=-1, keepdims=True)
    o_ref[...] = (o * lax.rsqrt(ms + EPS) * sg_ref[...] * (1.0 - lambda_init)).astype(o_ref.dtype)


def _diff_attention(r1, pv, lam, subln_g, lambda_init, *, tq):
    b, s, _ = r1.shape
    return pl.pallas_call(
        functools.partial(_diff_attn_kernel, lambda_init=lambda_init),
        out_shape=jax.ShapeDtypeStruct((b, s, BRANCH_W), BF16),
        grid=(b, DA_HEADS, s // tq),
        in_specs=[pl.BlockSpec((None, tq, LANES), lambda bi, h, i: (bi, i, h)),
                  pl.BlockSpec((None, s, LANES), lambda bi, h, i: (bi, 0, 4 + h)),
                  pl.BlockSpec((None, s, LANES), lambda bi, h, i: (bi, 0, 4 + h)),
                  pl.BlockSpec((4, HEAD_DIM), lambda bi, h, i: (0, 0)),
                  pl.BlockSpec((1, LANES), lambda bi, h, i: (0, 0))],
        out_specs=pl.BlockSpec((None, tq, LANES), lambda bi, h, i: (bi, i, h)),
        compiler_params=_cparams(3),
        name="diff_attn",
    )(r1, r1, pv, lam, subln_g)


def _gqa_kernel(q_ref, k_ref, v_ref, o_ref):
    tq = q_ref.shape[0]
    lo = _half_masks(tq)
    k = k_ref[...]
    v = v_ref[...]
    for c in range(q_ref.shape[1] // LANES):
        outs = []
        for qh in _masked_q(q_ref[:, c * LANES:(c + 1) * LANES], lo):
            s = _scores(qh, k)
            e = jnp.exp(s - jnp.max(s, axis=-1, keepdims=True))
            r = 1.0 / jnp.sum(e, axis=-1, keepdims=True)
            outs.append(jnp.dot(e.astype(BF16), v, preferred_element_type=F32) * r)
        o_ref[:, c * LANES:(c + 1) * LANES] = jnp.where(lo, outs[0], outs[1]).astype(o_ref.dtype)


def _gqa_attention(r2, pv, *, tq):
    b, s, _ = r2.shape
    qw = GQ_HEADS // GQ_KV * HEAD_DIM
    return pl.pallas_call(
        _gqa_kernel,
        out_shape=jax.ShapeDtypeStruct((b, s, BRANCH_W), BF16),
        grid=(b, GQ_KV, s // tq),
        in_specs=[pl.BlockSpec((None, tq, qw), lambda bi, g, i: (bi, i, g)),
                  pl.BlockSpec((None, s, LANES), lambda bi, g, i: (bi, 0, 4 + g)),
                  pl.BlockSpec((None, s, LANES), lambda bi, g, i: (bi, 0, 8 + g))],
        out_specs=pl.BlockSpec((None, tq, qw), lambda bi, g, i: (bi, i, g)),
        compiler_params=_cparams(3),
        name="gqa_attn",
    )(r2, r2, pv)


def _band_kernel(q_ref, k_ref, v_ref, o_ref, lse_ref, *, tq, span):
    sub_len = k_ref.shape[0]
    seg_len = q_ref.shape[0]
    base = pl.program_id(2) * seg_len
    lo = _half_masks(tq)
    q_pos = lax.broadcasted_iota(jnp.int32, (tq, span), 0)
    k_pos = lax.broadcasted_iota(jnp.int32, (tq, span), 1)

    def block(ib, carry):
        i0 = pl.multiple_of(ib * tq, tq)
        k0 = pl.multiple_of(jnp.clip(base + i0 - DIL_RADIUS, 0, sub_len - span), DIL_RADIUS)
        ok = jnp.abs((k_pos + k0) - (q_pos + (base + i0))) <= DIL_RADIUS
        for hp in range(DIL_HEADS // 2):
            cols = slice(hp * LANES, (hp + 1) * LANES)
            k = k_ref[pl.ds(k0, span), cols]
            v = v_ref[pl.ds(k0, span), cols]
            outs, lses = [], []
            for qh in _masked_q(q_ref[pl.ds(i0, tq), cols], lo):
                s = jnp.where(ok, _scores(qh, k), NEG_INF)
                m = jnp.max(s, axis=-1, keepdims=True)
                e = jnp.exp(s - m)
                l = jnp.sum(e, axis=-1, keepdims=True)
                outs.append(jnp.dot(e.astype(BF16), v, preferred_element_type=F32) * (1.0 / l))
                lses.append(jnp.broadcast_to(m + jnp.log(l), (tq, LANES)))
            o_ref[pl.ds(i0, tq), cols] = jnp.where(lo, outs[0], outs[1]).astype(o_ref.dtype)
            lse_ref[pl.ds(i0, tq), cols] = jnp.where(lo, lses[0], lses[1])
        return carry

    lax.fori_loop(0, seg_len // tq, block, 0)


def _dilated_group(qkv):
    b, dil, sub_len, _ = qkv.shape
    tq = min(128, sub_len)
    span = min(tq + 2 * DIL_RADIUS, sub_len)
    seg = min(BAND_SEGMENT, sub_len)
    q_spec = pl.BlockSpec((None, None, seg, BRANCH_W), lambda bi, r, sg: (bi, r, sg, 0))
    return pl.pallas_call(
        functools.partial(_band_kernel, tq=tq, span=span),
        out_shape=(jax.ShapeDtypeStruct((b, dil, sub_len, BRANCH_W), BF16),
                   jax.ShapeDtypeStruct((b, dil, sub_len, BRANCH_W), F32)),
        grid=(b, dil, sub_len // seg),
        in_specs=[q_spec,
                  pl.BlockSpec((None, None, sub_len, BRANCH_W), lambda bi, r, sg: (bi, r, 0, 1)),
                  pl.BlockSpec((None, None, sub_len, BRANCH_W), lambda bi, r, sg: (bi, r, 0, 2))],
        out_specs=(q_spec, q_spec),
        compiler_params=_cparams(3),
        name=f"band_attn_d{dil}",
    )(qkv, qkv, qkv)


def _nbr_kernel(q_ref, k_ref, v_ref, bias_ref, o_ref, *, rows):
    ib = pl.program_id(1)
    kbase = jnp.clip(ib * NBR_QROWS - NA_ROWS // 2, 0, rows - NBR_KROWS)
    t0 = pl.multiple_of(kbase * GRID_W, GRID_W)
    nk = NBR_KROWS * GRID_W
    lo = _half_masks(NBR_QROWS * GRID_W)
    for hp in range(NA_HEADS // 2):
        cols = slice(hp * LANES, (hp + 1) * LANES)
        k = k_ref[pl.ds(t0, nk), cols]
        v = v_ref[pl.ds(t0, nk), cols]
        outs = []
        for half, qh in enumerate(_masked_q(q_ref[:, cols], lo)):
            s = _scores(qh, k) + bias_ref[2 * hp + half]
            e = jnp.exp(s - jnp.max(s, axis=-1, keepdims=True))
            r = 1.0 / jnp.sum(e, axis=-1, keepdims=True)
            outs.append(jnp.dot(e.astype(BF16), v, preferred_element_type=F32) * r)
        o_ref[:, cols] = jnp.where(lo, outs[0], outs[1]).astype(o_ref.dtype)


def _nbr_bias_table(rpb, rows):
    assert rows >= NBR_KROWS and rows % NBR_QROWS == 0
    c = np.arange(GRID_W)[:, None]
    kc = np.arange(GRID_W)[None, :]
    win = np.clip(c - NA_COLS // 2, 0, GRID_W - NA_COLS)
    col_ok = (kc >= win) & (kc < win + NA_COLS)
    col_idx = np.clip(kc - c + NA_COLS - 1, 0, 2 * NA_COLS - 2)
    rel = np.zeros((3, NBR_QROWS, NBR_KROWS), np.int32)
    row_ok = np.zeros((3, NBR_QROWS, NBR_KROWS), bool)
    for vi, i0 in enumerate((0, NBR_QROWS, rows - NBR_QROWS)):
        krow = np.clip(i0 - NA_ROWS // 2, 0, rows - NBR_KROWS) + np.arange(NBR_KROWS)
        for ri in range(NBR_QROWS):
            rs = np.clip(i0 + ri - NA_ROWS // 2, 0, rows - NA_ROWS)
            row_ok[vi, ri] = (krow >= rs) & (krow < rs + NA_ROWS)
            rel[vi, ri] = np.clip(krow - (i0 + ri) + NA_ROWS - 1, 0, 2 * NA_ROWS - 2)
    t = rpb.astype(F32)[:, rel][..., col_idx]
    ok = row_ok[:, :, :, None, None] & col_ok[None, None, None]
    t = jnp.where(ok[None], t, NEG_INF)
    t = jnp.transpose(t, (1, 0, 2, 4, 3, 5))
    return t.reshape(3, NA_HEADS, NBR_QROWS * GRID_W, NBR_KROWS * GRID_W)


def _nbr_attention(r3, pv, bias_tbl):
    b, s, _ = r3.shape
    rows = s // GRID_W
    nblk = rows // NBR_QROWS
    tq, nk = NBR_QROWS * GRID_W, NBR_KROWS * GRID_W

    def variant(ib):
        return jnp.where(ib == 0, 0, jnp.where(ib == nblk - 1, 2, 1))

    return pl.pallas_call(
        functools.partial(_nbr_kernel, rows=rows),
        out_shape=jax.ShapeDtypeStruct((b, s, BRANCH_W), BF16),
        grid=(b, nblk),
        in_specs=[pl.BlockSpec((None, tq, BRANCH_W), lambda bi, ib: (bi, ib, 0)),
                  pl.BlockSpec((None, s, BRANCH_W), lambda bi, ib: (bi, 0, 1)),
                  pl.BlockSpec((None, s, BRANCH_W), lambda bi, ib: (bi, 0, 0)),
                  pl.BlockSpec((None, NA_HEADS, tq, nk), lambda bi, ib: (variant(ib), 0, 0, 0))],
        out_specs=pl.BlockSpec((None, tq, BRANCH_W), lambda bi, ib: (bi, ib, 0)),
        compiler_params=_cparams(2),
        name="nbr_attn",
    )(r3, r3, pv, bias_tbl)


def _merge_kernel(x_ref, oa_ref, ob_ref, oc0_ref, oc1_ref, oc2_ref, l0_ref, l1_ref, l2_ref, od_ref,
                  gate_ref, wb_ref, wo_ref, o_ref, *scr):
    tm = x_ref.shape[0]

    def natural(ref, buf):
        dil = ref.shape[0]
        if dil == 1:
            return ref[0].astype(F32)
        slabs = ref.shape[2] // LANES
        for rho in range(dil):
            val = ref[rho].astype(F32)
            for c in range(slabs):
                buf[c, pl.ds(rho, tm // dil, stride=dil), :] = val[:, c * LANES:(c + 1) * LANES]
        return jnp.concatenate([buf[c] for c in range(slabs)], axis=1)

    l0 = natural(l0_ref, None)
    l1, l2 = natural(l1_ref, scr[0]), natural(l2_ref, scr[1])
    m = jnp.maximum(jnp.maximum(l0, l1), l2)
    w0, w1, w2 = jnp.exp(l0 - m), jnp.exp(l1 - m), jnp.exp(l2 - m)
    oc = (w0 * natural(oc0_ref, None) + w1 * natural(oc1_ref, scr[2])
          + w2 * natural(oc2_ref, scr[3])) * (1.0 / (w0 + w1 + w2))
    branches = (oa_ref[...], ob_ref[...], oc.astype(BF16), od_ref[...])
    merged = None
    for bi, ob in enumerate(branches):
        t = (jnp.dot(ob, wb_ref[bi], preferred_element_type=F32)
             * gate_ref[:, bi * D_MODEL:(bi + 1) * D_MODEL].astype(F32))
        merged = t if merged is None else merged + t
    o_ref[...] = x_ref[...] + jnp.dot(merged.astype(BF16), wo_ref[...], preferred_element_type=F32)


def _merge(x2, oa, ob, ocs, lses, od, gate, wb, wo, *, tm, seq):
    t, d = x2.shape
    per_seq = seq // tm
    row = lambda w: pl.BlockSpec((tm, w), lambda i: (i, 0))

    def classes(a):
        dil = a.shape[1]
        assert tm % (dil * 16) == 0
        return pl.BlockSpec((None, dil, tm // dil, BRANCH_W), lambda i: (i // per_seq, 0, i % per_seq, 0))

    return pl.pallas_call(
        _merge_kernel,
        out_shape=jax.ShapeDtypeStruct((t, d), F32),
        grid=(t // tm,),
        in_specs=([row(d), row(BRANCH_W), row(BRANCH_W)] + [classes(a) for a in ocs]
                  + [classes(a) for a in lses]
                  + [row(BRANCH_W), row(N_BRANCH * d),
                     pl.BlockSpec((N_BRANCH, BRANCH_W, d), lambda i: (0, 0, 0)),
                     pl.BlockSpec((d, d), lambda i: (0, 0))]),
        out_specs=row(d),
        scratch_shapes=[pltpu.VMEM((BRANCH_W // LANES, tm, LANES), F32)] * 4,
        compiler_params=_cparams(1),
        name="merge",
    )(x2, oa, ob, *ocs, *lses, od, gate, wb, wo)


def _ffn_kernel(x_ref, xp_ref, xn_ref, g2_ref, wa_ref, wg_ref, cwa_ref, cwg_ref, cba_ref, cbg_ref,
                wd_ref, o_ref, h_scr, acc_scr, *, tiles_per_seq):
    i, j = pl.program_id(0), pl.program_id(1)
    tm = x_ref.shape[0]
    ext = tm + 2 * CONV_HALO

    @pl.when(j == 0)
    def _():
        g2 = g2_ref[...]

        def norm(x):
            return x * lax.rsqrt(jnp.mean(x * x, axis=-1, keepdims=True) + EPS) * g2

        first = (i % tiles_per_seq) == 0
        last = (i % tiles_per_seq) == tiles_per_seq - 1
        h_scr[0:CONV_HALO, :] = jnp.where(first, 0.0, norm(xp_ref[...])).astype(BF16)
        h_scr[CONV_HALO:CONV_HALO + tm, :] = norm(x_ref[...]).astype(BF16)
        h_scr[CONV_HALO + tm:ext, :] = jnp.where(last, 0.0, norm(xn_ref[...])).astype(BF16)
        acc_scr[...] = jnp.zeros_like(acc_scr)

    h = h_scr[...]

    def conv(w_ref, cw_ref, cb_ref):
        u = jnp.dot(h, w_ref[...], preferred_element_type=F32)
        prev = pltpu.roll(u, 1, 0)[CONV_HALO:CONV_HALO + tm]
        nxt = pltpu.roll(u, ext - 1, 0)[CONV_HALO:CONV_HALO + tm]
        cw = cw_ref[...]
        return (cw[0:1] * prev + cw[1:2] * u[CONV_HALO:CONV_HALO + tm] + cw[2:3] * nxt
                + cb_ref[...])

    a = conv(wa_ref, cwa_ref, cba_ref)
    g = conv(wg_ref, cwg_ref, cbg_ref)
    hm = (a * (1.0 / (1.0 + jnp.exp(-a))) * g).astype(BF16)
    acc_scr[...] += jnp.dot(hm, wd_ref[...], preferred_element_type=F32)

    @pl.when(j == pl.num_programs(1) - 1)
    def _():
        o_ref[...] = x_ref[...] + acc_scr[...]


def _ffn(x2, g2, w_up, conv_w, conv_b, w_down, *, tm, tn, seq):
    t, d = x2.shape
    nj = D_FF // tn
    assert D_FF % tn == 0 and seq % tm == 0 and tm % CONV_HALO == 0
    hb = tm // CONV_HALO
    n_halo = t // CONV_HALO
    return pl.pallas_call(
        functools.partial(_ffn_kernel, tiles_per_seq=seq // tm),
        out_shape=jax.ShapeDtypeStruct((t, d), F32),
        grid=(t // tm, nj),
        in_specs=[pl.BlockSpec((tm, d), lambda i, j: (i, 0)),
                  pl.BlockSpec((CONV_HALO, d), lambda i, j: (jnp.maximum(i * hb - 1, 0), 0)),
                  pl.BlockSpec((CONV_HALO, d), lambda i, j: (jnp.minimum((i + 1) * hb, n_halo - 1), 0)),
                  pl.BlockSpec((1, d), lambda i, j: (0, 0)),
                  pl.BlockSpec((d, tn), lambda i, j: (0, j)),
                  pl.BlockSpec((d, tn), lambda i, j: (0, nj + j)),
                  pl.BlockSpec((3, tn), lambda i, j: (0, j)),
                  pl.BlockSpec((3, tn), lambda i, j: (0, nj + j)),
                  pl.BlockSpec((1, tn), lambda i, j: (0, j)),
                  pl.BlockSpec((1, tn), lambda i, j: (0, nj + j)),
                  pl.BlockSpec((tn, d), lambda i, j: (j, 0))],
        out_specs=pl.BlockSpec((tm, d), lambda i, j: (i, 0)),
        scratch_shapes=[pltpu.VMEM((tm + 2 * CONV_HALO, d), BF16), pltpu.VMEM((tm, d), F32)],
        compiler_params=_cparams(2),
        name="conv_ffn",
    )(x2, x2, x2, g2, w_up, w_up, conv_w, conv_w, conv_b, conv_b, w_down)


def _rope_tables(seq):
    pos = jnp.arange(seq, dtype=jnp.int32)

    def angles(p, dim):
        inv = ROPE_THETA ** (-(jnp.arange(0, dim, 2, dtype=F32) / dim))
        return p.astype(F32)[:, None] * inv[None, :]

    def tables(ang_per_lane, first_half):
        c, s = jnp.cos(ang_per_lane), jnp.sin(ang_per_lane)
        sa = jnp.where(first_half[None, :], -s, 0.0)
        sb = jnp.where(first_half[None, :], 0.0, s)
        return tuple(jnp.tile(t, (1, LANES // HEAD_DIM)) for t in (c, sa, sb))

    lane = np.arange(HEAD_DIM)
    half = HEAD_DIM // 2
    a1 = angles(pos, HEAD_DIM)
    t1 = tables(jnp.concatenate([a1, a1], axis=1), lane < half)
    ar = angles(pos // GRID_W, half)
    ac = angles(pos % GRID_W, half)
    t2 = tables(jnp.concatenate([ar, ar, ac, ac], axis=1), (lane % half) < half // 2)
    return t1, t2


def _cols(w, name, part=None):
    o, n = _OFF[name]
    if part is not None:
        n //= len(DIL_GROUPS)
        o += part * n
    return w[:, o:o + n]


def _dup_heads(w):
    k0, k1 = w[:, :HEAD_DIM], w[:, HEAD_DIM:]
    return jnp.concatenate([k0, k0, k1, k1], axis=1)


def _head_gain(g, n_heads):
    return jnp.tile(g.astype(F32), n_heads)


def _by_residue(table, tm, dil):
    s = table.shape[0]
    return table.reshape(s // tm, tm // dil, dil, -1).transpose(0, 2, 1, 3).reshape(s, -1)


def kernel(x, norm1_g, w_in, qk_g, lam, subln_g, rpb, w_branch, w_out, norm2_g, w_up, conv_w, conv_b, w_down):
    b, s, d = x.shape
    depth = w_in.shape[0]
    t = b * s
    rows = s // GRID_W
    tm = min(TOKEN_TILE, s)
    t1d, tax = _rope_tables(s)
    t1d_dil = [tuple(_by_residue(tb, tm, dil) for tb in t1d) for _, dil in DIL_GROUPS]
    x2 = x.reshape(t, d)

    for l in range(depth):
        lambda_init = 0.8 - 0.6 * math.exp(-0.3 * l)
        w = w_in[l]
        w_r1 = jnp.concatenate([_cols(w, "aq"), _cols(w, "ak")], axis=1).astype(BF16)
        w_c = [jnp.concatenate([_cols(w, n, gi) for n in ("cq", "ck", "cv")], axis=1).astype(BF16)
               for gi in range(len(DIL_GROUPS))]
        w_r2 = jnp.concatenate([_cols(w, "bq"), _dup_heads(_cols(w, "bk"))], axis=1).astype(BF16)
        w_r3 = jnp.concatenate([_cols(w, "dq"), _cols(w, "dk")], axis=1).astype(BF16)
        w_pv = jnp.concatenate([_cols(w, "dv"), _cols(w, "av"), _dup_heads(_cols(w, "bv"))],
                               axis=1).astype(BF16)
        w_g = _cols(w, "g").astype(BF16)
        gq = qk_g[l]
        gain1 = jnp.concatenate([_head_gain(gq[0, 0], 8), _head_gain(gq[0, 1], 8)])[None, :]
        gain_c = jnp.concatenate([_head_gain(gq[2, 0], 8), _head_gain(gq[2, 1], 8),
                                  jnp.ones((BRANCH_W,), F32)])[None, :]
        gain2 = jnp.concatenate([_head_gain(gq[1, 0], 8), _head_gain(gq[1, 1], 4)])[None, :]
        gain3 = jnp.concatenate([_head_gain(gq[3, 0], 8), _head_gain(gq[3, 1], 8)])[None, :]
        g1 = norm1_g[l].astype(F32)[None, :]

        half = HEAD_DIM // 2
        r1 = _proj(x2, g1, w_r1, mode="rope", tm=tm, tn=512, seq=s, gain=gain1, tables=t1d,
                   shift=half).reshape(b, s, -1)
        qkv_c = [_proj(x2, g1, w_c[gi], mode="rope", tm=tm, tn=BRANCH_W, seq=s, gain=gain_c,
                       tables=t1d_dil[gi], shift=half, dil=dil, rope_blocks=2)
                 for gi, (_, dil) in enumerate(DIL_GROUPS)]
        r2 = _proj(x2, g1, w_r2, mode="rope", tm=tm, tn=768, seq=s, gain=gain2, tables=tax,
                   shift=half // 2).reshape(b, s, -1)
        r3 = _proj(x2, g1, w_r3, mode="norm", tm=tm, tn=512, seq=s, gain=gain3).reshape(b, s, -1)
        pv = _proj(x2, g1, w_pv, mode="plain", tm=tm, tn=1280, seq=s).reshape(b, s, -1)
        gate = _proj(x2, g1, w_g, mode="sigmoid", tm=tm, tn=512, seq=s)

        o_a = _diff_attention(r1, pv, lam[l].astype(F32), subln_g[l].astype(F32)[None, :],
                              lambda_init, tq=min(ATTN_Q_TILE, s))
        o_b = _gqa_attention(r2, pv, tq=min(ATTN_Q_TILE, s))
        ocs, lses = zip(*[_dilated_group(qkv) for qkv in qkv_c])
        o_d = _nbr_attention(r3, pv, _nbr_bias_table(rpb[l], rows))

        x2 = _merge(x2, o_a.reshape(t, BRANCH_W), o_b.reshape(t, BRANCH_W), ocs, lses,
                    o_d.reshape(t, BRANCH_W), gate, w_branch[l].astype(BF16), w_out[l].astype(BF16),
                    tm=min(MERGE_TILE, s), seq=s)
        x2 = _ffn(x2, norm2_g[l].astype(F32)[None, :], w_up[l].astype(BF16), conv_w[l].astype(F32),
                  conv_b[l].astype(F32)[None, :], w_down[l].astype(BF16), tm=tm, tn=FFN_COL_TILE, seq=s)
    return x2.reshape(b, s, d)
```

```python
import functools
import math

import jax
import jax.numpy as jnp
import numpy as np
from jax import lax
from jax.experimental import pallas as pl
from jax.experimental.pallas import tpu as pltpu

D_MODEL = 1024
HEAD_DIM = 64
LANES = 128
MXU_COLS = 256
ROPE_THETA = 10000.0
GRID_W = 64
EPS = 1e-6
NEG_INF = -1e30
SCALE = HEAD_DIM ** -0.5

DA_HEADS = 4
DA_HEADS_PER_STEP = 2
GQ_HEADS = 8
GQ_KV = 2
DIL_GROUPS = ((128, 1), (512, 4), (2048, 16))
DIL_HEADS = 8
DIL_RADIUS = 64
NA_HEADS = 8
NA_ROWS = 8
NA_COLS = 16
N_BRANCH = 4
BRANCH_W = 512
D_FF = 2816
CONV_HALO = 16
TOKEN_TILE = 1024
PROJ_ROW_CHUNK = 512
ATTN_Q_TILE = 256
MERGE_TILE = 512
FFN_COL_TILE = 256
BAND_SEGMENT = 1024
NBR_QROWS = 4
NBR_KROWS = NBR_QROWS + NA_ROWS

_OFF = {}
_o = 0
for _name, _n in (("aq", 512), ("ak", 512), ("av", 512), ("bq", 512), ("bk", 128), ("bv", 128),
                  ("cq", 1536), ("ck", 1536), ("cv", 1536), ("dq", 512), ("dk", 512), ("dv", 512),
                  ("g", 4096)):
    _OFF[_name] = (_o, _n)
    _o += _n

VMEM_LIMIT = 56 * 1024 * 1024

BF16 = jnp.bfloat16
F32 = jnp.float32


def _cparams(n_axes):
    return pltpu.CompilerParams(dimension_semantics=("arbitrary",) * n_axes,
                                vmem_limit_bytes=VMEM_LIMIT)


def _group_sumsq(y, bd):
    sq = y * y
    hi = sq.astype(BF16)
    lo = (sq - hi.astype(F32)).astype(BF16)
    return (jnp.dot(hi, bd, preferred_element_type=F32)
            + jnp.dot(lo, bd, preferred_element_type=F32))


def _proj_kernel(*refs, mode, tn, shift, dil, rope_blocks):
    xs_scr = None
    if dil > 1:
        refs, xs_scr = refs[:-1], refs[-1]
    if mode == "rope":
        x_ref, g1_ref, w_ref, gain_ref, bd_ref, c_ref, sa_ref, sb_ref, o_ref, h_scr = refs
    elif mode == "norm":
        x_ref, g1_ref, w_ref, gain_ref, bd_ref, o_ref, h_scr = refs
    else:
        x_ref, g1_ref, w_ref, o_ref, h_scr = refs
    j = pl.program_id(1)
    sub = x_ref.shape[0] // dil
    slabs = x_ref.shape[1] // LANES

    @pl.when(j == 0)
    def _():
        g1 = g1_ref[...]
        if dil > 1:
            for c in range(slabs):
                xs_scr[c] = x_ref[:, c * LANES:(c + 1) * LANES]
        for rho in range(dil):
            if dil == 1:
                x = x_ref[...]
            else:
                x = jnp.concatenate([xs_scr[c, pl.ds(rho, sub, stride=dil), :] for c in range(slabs)], axis=1)
            ms = jnp.mean(x * x, axis=-1, keepdims=True)
            h_scr[rho * sub:(rho + 1) * sub, :] = (x * lax.rsqrt(ms + EPS) * g1).astype(BF16)

    tm = x_ref.shape[0]
    chunk = min(PROJ_ROW_CHUNK, tm)
    w = w_ref[...]

    def store(r0, val):
        val = val.astype(o_ref.dtype)
        if len(o_ref.shape) == 2:
            o_ref[r0:r0 + chunk, :] = val
        else:
            piece = min(chunk, sub)
            for p in range(r0, r0 + chunk, piece):
                o_ref[p // sub, p % sub:p % sub + piece, :] = val[p - r0:p - r0 + piece]

    def normed(r0, y):
        bd = bd_ref[...]
        parts = []
        for c in range(tn // MXU_COLS):
            yc = y[:, c * MXU_COLS:(c + 1) * MXU_COLS]
            ss = _group_sumsq(yc, bd)
            parts.append(yc * lax.rsqrt(ss * (1.0 / HEAD_DIM) + EPS))
        n = (parts[0] if len(parts) == 1 else jnp.concatenate(parts, axis=1)) * gain_ref[...]
        if mode == "rope":
            reps = tn // LANES
            cos = jnp.concatenate([c_ref[r0:r0 + chunk, :]] * reps, axis=1)
            sa = jnp.concatenate([sa_ref[r0:r0 + chunk, :]] * reps, axis=1)
            sb = jnp.concatenate([sb_ref[r0:r0 + chunk, :]] * reps, axis=1)
            n = (n * cos + pltpu.roll(n, tn - shift, 1) * sa + pltpu.roll(n, shift, 1) * sb)
        return n

    def sweep(epilogue):
        for r0 in range(0, tm, chunk):
            y = jnp.dot(h_scr[r0:r0 + chunk, :], w, preferred_element_type=F32)
            store(r0, epilogue(r0, y))

    plain = lambda r0, y: y
    if mode == "plain":
        sweep(plain)
    elif mode == "sigmoid":
        sweep(lambda r0, y: 0.5 * jnp.tanh(0.5 * y) + 0.5)
    elif rope_blocks is None:
        sweep(normed)
    else:
        pl.when(j < rope_blocks)(lambda: sweep(normed))
        pl.when(j >= rope_blocks)(lambda: sweep(plain))


def _proj(x2, g1, w, *, mode, tm, tn, seq, gain=None, tables=None, shift=0, dil=None, rope_blocks=None):
    t, d = x2.shape
    n = w.shape[1]
    per_seq = seq // tm
    assert t % tm == 0 and n % tn == 0 and seq % tm == 0
    in_specs = [pl.BlockSpec((tm, d), lambda i, j: (i, 0)),
                pl.BlockSpec((1, d), lambda i, j: (0, 0)),
                pl.BlockSpec((d, tn), lambda i, j: (0, j))]
    args = [x2, g1, w]
    if mode in ("rope", "norm"):
        assert tn % MXU_COLS == 0
        blk = np.kron(np.eye(MXU_COLS // HEAD_DIM), np.ones((HEAD_DIM, HEAD_DIM)))
        in_specs += [pl.BlockSpec((1, tn), lambda i, j: (0, j)),
                     pl.BlockSpec((MXU_COLS, MXU_COLS), lambda i, j: (0, 0))]
        args += [gain, jnp.asarray(blk, BF16)]
    if mode == "rope":
        for tb in tables:
            in_specs.append(pl.BlockSpec((tm, LANES), lambda i, j: (i % per_seq, 0)))
            args.append(tb)
    if dil is None:
        out_shape = jax.ShapeDtypeStruct((t, n), BF16)
        out_spec = pl.BlockSpec((tm, tn), lambda i, j: (i, j))
    else:
        assert tm % (dil * 16) == 0
        out_shape = jax.ShapeDtypeStruct((t // seq, dil, seq // dil, n), BF16)
        out_spec = pl.BlockSpec((None, dil, tm // dil, tn), lambda i, j: (i // per_seq, 0, i % per_seq, j))
    scratch = [pltpu.VMEM((tm, d), BF16)]
    if dil is not None and dil > 1:
        scratch.append(pltpu.VMEM((d // LANES, tm, LANES), F32))
    return pl.pallas_call(
        functools.partial(_proj_kernel, mode=mode, tn=tn, shift=shift, dil=dil or 1, rope_blocks=rope_blocks),
        out_shape=out_shape,
        grid=(t // tm, n // tn),
        in_specs=in_specs,
        out_specs=out_spec,
        scratch_shapes=scratch,
        compiler_params=_cparams(2),
        name=f"proj_{mode}{shift}" + (f"_d{dil}" if dil else ""),
    )(*args)


def _half_masks(rows):
    lane = lax.broadcasted_iota(jnp.int32, (rows, LANES), 1)
    return lane < HEAD_DIM


def _masked_q(q_ref_val, lo_mask):
    qf = q_ref_val.astype(F32) * SCALE
    q_lo = jnp.where(lo_mask, qf, 0.0).astype(BF16)
    q_hi = jnp.where(lo_mask, 0.0, qf).astype(BF16)
    return q_lo, q_hi


def _scores(q, k):
    return lax.dot_general(q, k, (((1,), (1,)), ((), ())), preferred_element_type=F32)


def _diff_attn_kernel(q_ref, k_ref, v_ref, lam_ref, sg_ref, o_ref, *, lambda_init):
    tq = q_ref.shape[0]
    lo = _half_masks(tq)
    lam = lam_ref[...]
    lam_val = (jnp.exp(jnp.sum(lam[0:1] * lam[1:2], axis=-1, keepdims=True))
               - jnp.exp(jnp.sum(lam[2:3] * lam[3:4], axis=-1, keepdims=True)) + lambda_init)
    for hh in range(q_ref.shape[1] // LANES):
        cols = slice(hh * LANES, (hh + 1) * LANES)
        q1, q2 = _masked_q(q_ref[:, cols], lo)
        k = k_ref[:, cols]
        s1 = _scores(q1, k)
        e1 = jnp.exp(s1 - jnp.max(s1, axis=-1, keepdims=True))
        r1 = 1.0 / jnp.sum(e1, axis=-1, keepdims=True)
        s2 = _scores(q2, k)
        e2 = jnp.exp(s2 - jnp.max(s2, axis=-1, keepdims=True))
        r2 = lam_val / jnp.sum(e2, axis=-1, keepdims=True)
        a = (e1 * r1 - e2 * r2).astype(BF16)
        o = jnp.dot(a, v_ref[:, cols], preferred_element_type=F32)
        ms = jnp.mean(o * o, axis=-1, keepdims=True)
        o_ref[:, cols] = (o * lax.rsqrt(ms + EPS) * sg_ref[...] * (1.0 - lambda_init)).astype(o_ref.dtype)


def _diff_attention(r1, pv, lam, subln_g, lambda_init, *, tq):
    b, s, _ = r1.shape
    hw = DA_HEADS_PER_STEP * LANES
    first_k = DA_HEADS // DA_HEADS_PER_STEP
    return pl.pallas_call(
        functools.partial(_diff_attn_kernel, lambda_init=lambda_init),
        out_shape=jax.ShapeDtypeStruct((b, s, BRANCH_W), BF16),
        grid=(b, DA_HEADS // DA_HEADS_PER_STEP, s // tq),
        in_specs=[pl.BlockSpec((None, tq, hw), lambda bi, h, i: (bi, i, h)),
                  pl.BlockSpec((None, s, hw), lambda bi, h, i: (bi, 0, first_k + h)),
                  pl.BlockSpec((None, s, hw), lambda bi, h, i: (bi, 0, first_k + h)),
                  pl.BlockSpec((4, HEAD_DIM), lambda bi, h, i: (0, 0)),
                  pl.BlockSpec((1, LANES), lambda bi, h, i: (0, 0))],
        out_specs=pl.BlockSpec((None, tq, hw), lambda bi, h, i: (bi, i, h)),
        compiler_params=_cparams(3),
        name="diff_attn",
    )(r1, r1, pv, lam, subln_g)


def _gqa_kernel(q_ref, k_ref, v_ref, o_ref):
    tq = q_ref.shape[0]
    lo = _half_masks(tq)
    k = k_ref[...]
    v = v_ref[...]
    for c in range(q_ref.shape[1] // LANES):
        outs = []
        for qh in _masked_q(q_ref[:, c * LANES:(c + 1) * LANES], lo):
            s = _scores(qh, k)
            e = jnp.exp(s - jnp.max(s, axis=-1, keepdims=True))
            r = 1.0 / jnp.sum(e, axis=-1, keepdims=True)
            outs.append(jnp.dot(e.astype(BF16), v, preferred_element_type=F32) * r)
        o_ref[:, c * LANES:(c + 1) * LANES] = jnp.where(lo, outs[0], outs[1]).astype(o_ref.dtype)


def _gqa_attention(r2, pv, *, tq):
    b, s, _ = r2.shape
    qw = GQ_HEADS // GQ_KV * HEAD_DIM
    return pl.pallas_call(
        _gqa_kernel,
        out_shape=jax.ShapeDtypeStruct((b, s, BRANCH_W), BF16),
        grid=(b, GQ_KV, s // tq),
        in_specs=[pl.BlockSpec((None, tq, qw), lambda bi, g, i: (bi, i, g)),
                  pl.BlockSpec((None, s, LANES), lambda bi, g, i: (bi, 0, 4 + g)),
                  pl.BlockSpec((None, s, LANES), lambda bi, g, i: (bi, 0, 8 + g))],
        out_specs=pl.BlockSpec((None, tq, qw), lambda bi, g, i: (bi, i, g)),
        compiler_params=_cparams(3),
        name="gqa_attn",
    )(r2, r2, pv)


def _band_kernel(q_ref, k_ref, v_ref, o_ref, lse_ref, *, tq, span):
    sub_len = k_ref.shape[0]
    seg_len = q_ref.shape[0]
    base = pl.program_id(2) * seg_len
    lo = _half_masks(tq)
    q_pos = lax.broadcasted_iota(jnp.int32, (tq, span), 0)
    k_pos = lax.broadcasted_iota(jnp.int32, (tq, span), 1)

    def block(ib, carry):
        i0 = pl.multiple_of(ib * tq, tq)
        k0 = pl.multiple_of(jnp.clip(base + i0 - DIL_RADIUS, 0, sub_len - span), DIL_RADIUS)
        ok = jnp.abs((k_pos + k0) - (q_pos + (base + i0))) <= DIL_RADIUS
        for hp in range(DIL_HEADS // 2):
            cols = slice(hp * LANES, (hp + 1) * LANES)
            k = k_ref[pl.ds(k0, span), cols]
            v = v_ref[pl.ds(k0, span), cols]
            outs, lses = [], []
            for qh in _masked_q(q_ref[pl.ds(i0, tq), cols], lo):
                s = jnp.where(ok, _scores(qh, k), NEG_INF)
                m = jnp.max(s, axis=-1, keepdims=True)
                e = jnp.exp(s - m)
                l = jnp.sum(e, axis=-1, keepdims=True)
                outs.append(jnp.dot(e.astype(BF16), v, preferred_element_type=F32) * (1.0 / l))
                lses.append(jnp.broadcast_to(m + jnp.log(l), (tq, LANES)))
            o_ref[pl.ds(i0, tq), cols] = jnp.where(lo, outs[0], outs[1]).astype(o_ref.dtype)
            lse_ref[pl.ds(i0, tq), cols] = jnp.where(lo, lses[0], lses[1])
        return carry

    lax.fori_loop(0, seg_len // tq, block, 0)


def _dilated_group(qkv):
    b, dil, sub_len, _ = qkv.shape
    tq = min(128, sub_len)
    span = min(tq + 2 * DIL_RADIUS, sub_len)
    seg = min(BAND_SEGMENT, sub_len)
    q_spec = pl.BlockSpec((None, None, seg, BRANCH_W), lambda bi, r, sg: (bi, r, sg, 0))
    return pl.pallas_call(
        functools.partial(_band_kernel, tq=tq, span=span),
        out_shape=(jax.ShapeDtypeStruct((b, dil, sub_len, BRANCH_W), BF16),
                   jax.ShapeDtypeStruct((b, dil, sub_len, BRANCH_W), F32)),
        grid=(b, dil, sub_len // seg),
        in_specs=[q_spec,
                  pl.BlockSpec((None, None, sub_len, BRANCH_W), lambda bi, r, sg: (bi, r, 0, 1)),
                  pl.BlockSpec((None, None, sub_len, BRANCH_W), lambda bi, r, sg: (bi, r, 0, 2))],
        out_specs=(q_spec, q_spec),
        compiler_params=_cparams(3),
        name=f"band_attn_d{dil}",
    )(qkv, qkv, qkv)


def _nbr_kernel(q_ref, k_ref, v_ref, bias_ref, o_ref, *, rows):
    ib = pl.program_id(1)
    kbase = jnp.clip(ib * NBR_QROWS - NA_ROWS // 2, 0, rows - NBR_KROWS)
    t0 = pl.multiple_of(kbase * GRID_W, GRID_W)
    nk = NBR_KROWS * GRID_W
    lo = _half_masks(NBR_QROWS * GRID_W)
    for hp in range(NA_HEADS // 2):
        cols = slice(hp * LANES, (hp + 1) * LANES)
        k = k_ref[pl.ds(t0, nk), cols]
        v = v_ref[pl.ds(t0, nk), cols]
        outs = []
        for half, qh in enumerate(_masked_q(q_ref[:, cols], lo)):
            s = _scores(qh, k) + bias_ref[2 * hp + half]
            e = jnp.exp(s - jnp.max(s, axis=-1, keepdims=True))
            r = 1.0 / jnp.sum(e, axis=-1, keepdims=True)
            outs.append(jnp.dot(e.astype(BF16), v, preferred_element_type=F32) * r)
        o_ref[:, cols] = jnp.where(lo, outs[0], outs[1]).astype(o_ref.dtype)


def _nbr_bias_table(rpb, rows):
    assert rows >= NBR_KROWS and rows % NBR_QROWS == 0
    c = np.arange(GRID_W)[:, None]
    kc = np.arange(GRID_W)[None, :]
    win = np.clip(c - NA_COLS // 2, 0, GRID_W - NA_COLS)
    col_ok = (kc >= win) & (kc < win + NA_COLS)
    col_idx = np.clip(kc - c + NA_COLS - 1, 0, 2 * NA_COLS - 2)
    rel = np.zeros((3, NBR_QROWS, NBR_KROWS), np.int32)
    row_ok = np.zeros((3, NBR_QROWS, NBR_KROWS), bool)
    for vi, i0 in enumerate((0, NBR_QROWS, rows - NBR_QROWS)):
        krow = np.clip(i0 - NA_ROWS // 2, 0, rows - NBR_KROWS) + np.arange(NBR_KROWS)
        for ri in range(NBR_QROWS):
            rs = np.clip(i0 + ri - NA_ROWS // 2, 0, rows - NA_ROWS)
            row_ok[vi, ri] = (krow >= rs) & (krow < rs + NA_ROWS)
            rel[vi, ri] = np.clip(krow - (i0 + ri) + NA_ROWS - 1, 0, 2 * NA_ROWS - 2)
    t = rpb.astype(F32)[:, rel][..., col_idx]
    ok = row_ok[:, :, :, None, None] & col_ok[None, None, None]
    t = jnp.where(ok[None], t, NEG_INF)
    t = jnp.transpose(t, (1, 0, 2, 4, 3, 5))
    return t.reshape(3, NA_HEADS, NBR_QROWS * GRID_W, NBR_KROWS * GRID_W)


def _nbr_attention(r3, pv, bias_tbl):
    b, s, _ = r3.shape
    rows = s // GRID_W
    nblk = rows // NBR_QROWS
    tq, nk = NBR_QROWS * GRID_W, NBR_KROWS * GRID_W

    def variant(ib):
        return jnp.where(ib == 0, 0, jnp.where(ib == nblk - 1, 2, 1))

    return pl.pallas_call(
        functools.partial(_nbr_kernel, rows=rows),
        out_shape=jax.ShapeDtypeStruct((b, s, BRANCH_W), BF16),
        grid=(b, nblk),
        in_specs=[pl.BlockSpec((None, tq, BRANCH_W), lambda bi, ib: (bi, ib, 0)),
                  pl.BlockSpec((None, s, BRANCH_W), lambda bi, ib: (bi, 0, 1)),
                  pl.BlockSpec((None, s, BRANCH_W), lambda bi, ib: (bi, 0, 0)),
                  pl.BlockSpec((None, NA_HEADS, tq, nk), lambda bi, ib: (variant(ib), 0, 0, 0))],
        out_specs=pl.BlockSpec((None, tq, BRANCH_W), lambda bi, ib: (bi, ib, 0)),
        compiler_params=_cparams(2),
        name="nbr_attn",
    )(r3, r3, pv, bias_tbl)


def _merge_kernel(x_ref, oa_ref, ob_ref, oc0_ref, oc1_ref, oc2_ref, l0_ref, l1_ref, l2_ref, od_ref,
                  gate_ref, wb_ref, wo_ref, o_ref, *scr):
    tm = x_ref.shape[0]

    def natural(ref, buf):
        dil = ref.shape[0]
        if dil == 1:
            return ref[0].astype(F32)
        slabs = ref.shape[2] // LANES
        for rho in range(dil):
            val = ref[rho].astype(F32)
            for c in range(slabs):
                buf[c, pl.ds(rho, tm // dil, stride=dil), :] = val[:, c * LANES:(c + 1) * LANES]
        return jnp.concatenate([buf[c] for c in range(slabs)], axis=1)

    l0 = natural(l0_ref, None)
    l1, l2 = natural(l1_ref, scr[0]), natural(l2_ref, scr[1])
    m = jnp.maximum(jnp.maximum(l0, l1), l2)
    w0, w1, w2 = jnp.exp(l0 - m), jnp.exp(l1 - m), jnp.exp(l2 - m)
    oc = (w0 * natural(oc0_ref, None) + w1 * natural(oc1_ref, scr[2])
          + w2 * natural(oc2_ref, scr[3])) * (1.0 / (w0 + w1 + w2))
    branches = (oa_ref[...], ob_ref[...], oc.astype(BF16), od_ref[...])
    merged = None
    for bi, ob in enumerate(branches):
        t = (jnp.dot(ob, wb_ref[bi], preferred_element_type=F32)
             * gate_ref[:, bi * D_MODEL:(bi + 1) * D_MODEL].astype(F32))
        merged = t if merged is None else merged + t
    o_ref[...] = x_ref[...] + jnp.dot(merged.astype(BF16), wo_ref[...], preferred_element_type=F32)


def _merge(x2, oa, ob, ocs, lses, od, gate, wb, wo, *, tm, seq):
    t, d = x2.shape
    per_seq = seq // tm
    row = lambda w: pl.BlockSpec((tm, w), lambda i: (i, 0))

    def classes(a):
        dil = a.shape[1]
        assert tm % (dil * 16) == 0
        return pl.BlockSpec((None, dil, tm // dil, BRANCH_W), lambda i: (i // per_seq, 0, i % per_seq, 0))

    return pl.pallas_call(
        _merge_kernel,
        out_shape=jax.ShapeDtypeStruct((t, d), F32),
        grid=(t // tm,),
        in_specs=([row(d), row(BRANCH_W), row(BRANCH_W)] + [classes(a) for a in ocs]
                  + [classes(a) for a in lses]
                  + [row(BRANCH_W), row(N_BRANCH * d),
                     pl.BlockSpec((N_BRANCH, BRANCH_W, d), lambda i: (0, 0, 0)),
                     pl.BlockSpec((d, d), lambda i: (0, 0))]),
        out_specs=row(d),
        scratch_shapes=[pltpu.VMEM((BRANCH_W // LANES, tm, LANES), F32)] * 4,
        compiler_params=_cparams(1),
        name="merge",
    )(x2, oa, ob, *ocs, *lses, od, gate, wb, wo)


def _ffn_kernel(x_ref, xp_ref, xn_ref, g2_ref, wa_ref, wg_ref, cwa_ref, cwg_ref, cba_ref, cbg_ref,
                wd_ref, o_ref, h_scr, acc_scr, *, tiles_per_seq):
    i, j = pl.program_id(0), pl.program_id(1)
    tm = x_ref.shape[0]
    ext = tm + 2 * CONV_HALO

    @pl.when(j == 0)
    def _():
        g2 = g2_ref[...]

        def norm(x):
            return x * lax.rsqrt(jnp.mean(x * x, axis=-1, keepdims=True) + EPS) * g2

        first = (i % tiles_per_seq) == 0
        last = (i % tiles_per_seq) == tiles_per_seq - 1
        h_scr[0:CONV_HALO, :] = jnp.where(first, 0.0, norm(xp_ref[...])).astype(BF16)
        h_scr[CONV_HALO:CONV_HALO + tm, :] = norm(x_ref[...]).astype(BF16)
        h_scr[CONV_HALO + tm:ext, :] = jnp.where(last, 0.0, norm(xn_ref[...])).astype(BF16)
        acc_scr[...] = jnp.zeros_like(acc_scr)

    h = h_scr[...]

    def conv(w_ref, cw_ref, cb_ref):
        u = jnp.dot(h, w_ref[...], preferred_element_type=F32)
        prev = pltpu.roll(u, 1, 0)[CONV_HALO:CONV_HALO + tm]
        nxt = pltpu.roll(u, ext - 1, 0)[CONV_HALO:CONV_HALO + tm]
        cw = cw_ref[...]
        return (cw[0:1] * prev + cw[1:2] * u[CONV_HALO:CONV_HALO + tm] + cw[2:3] * nxt
                + cb_ref[...])

    a = conv(wa_ref, cwa_ref, cba_ref)
    g = conv(wg_ref, cwg_ref, cbg_ref)
    hm = (a * (1.0 / (1.0 + jnp.exp(-a))) * g).astype(BF16)
    acc_scr[...] += jnp.dot(hm, wd_ref[...], preferred_element_type=F32)

    @pl.when(j == pl.num_programs(1) - 1)
    def _():
        o_ref[...] = x_ref[...] + acc_scr[...]


def _ffn(x2, g2, w_up, conv_w, conv_b, w_down, *, tm, tn, seq):
    t, d = x2.shape
    nj = D_FF // tn
    assert D_FF % tn == 0 and seq % tm == 0 and tm % CONV_HALO == 0
    hb = tm // CONV_HALO
    n_halo = t // CONV_HALO
    return pl.pallas_call(
        functools.partial(_ffn_kernel, tiles_per_seq=seq // tm),
        out_shape=jax.ShapeDtypeStruct((t, d), F32),
        grid=(t // tm, nj),
        in_specs=[pl.BlockSpec((tm, d), lambda i, j: (i, 0)),
                  pl.BlockSpec((CONV_HALO, d), lambda i, j: (jnp.maximum(i * hb - 1, 0), 0)),
                  pl.BlockSpec((CONV_HALO, d), lambda i, j: (jnp.minimum((i + 1) * hb, n_halo - 1), 0)),
                  pl.BlockSpec((1, d), lambda i, j: (0, 0)),
                  pl.BlockSpec((d, tn), lambda i, j: (0, j)),
                  pl.BlockSpec((d, tn), lambda i, j: (0, nj + j)),
                  pl.BlockSpec((3, tn), lambda i, j: (0, j)),
                  pl.BlockSpec((3, tn), lambda i, j: (0, nj + j)),
                  pl.BlockSpec((1, tn), lambda i, j: (0, j)),
                  pl.BlockSpec((1, tn), lambda i, j: (0, nj + j)),
                  pl.BlockSpec((tn, d), lambda i, j: (j, 0))],
        out_specs=pl.BlockSpec((tm, d), lambda i, j: (i, 0)),
        scratch_shapes=[pltpu.VMEM((tm + 2 * CONV_HALO, d), BF16), pltpu.VMEM((tm, d), F32)],
        compiler_params=_cparams(2),
        name="conv_ffn",
    )(x2, x2, x2, g2, w_up, w_up, conv_w, conv_w, conv_b, conv_b, w_down)


def _rope_tables(seq):
    pos = jnp.arange(seq, dtype=jnp.int32)

    def angles(p, dim):
        inv = ROPE_THETA ** (-(jnp.arange(0, dim, 2, dtype=F32) / dim))
        return p.astype(F32)[:, None] * inv[None, :]

    def tables(ang_per_lane, first_half):
        c, s = jnp.cos(ang_per_lane), jnp.sin(ang_per_lane)
        sa = jnp.where(first_half[None, :], -s, 0.0)
        sb = jnp.where(first_half[None, :], 0.0, s)
        return tuple(jnp.tile(t, (1, LANES // HEAD_DIM)) for t in (c, sa, sb))

    lane = np.arange(HEAD_DIM)
    half = HEAD_DIM // 2
    a1 = angles(pos, HEAD_DIM)
    t1 = tables(jnp.concatenate([a1, a1], axis=1), lane < half)
    ar = angles(pos // GRID_W, half)
    ac = angles(pos % GRID_W, half)
    t2 = tables(jnp.concatenate([ar, ar, ac, ac], axis=1), (lane % half) < half // 2)
    return t1, t2


def _cols(w, name, part=None):
    o, n = _OFF[name]
    if part is not None:
        n //= len(DIL_GROUPS)
        o += part * n
    return w[:, o:o + n]


def _dup_heads(w):
    k0, k1 = w[:, :HEAD_DIM], w[:, HEAD_DIM:]
    return jnp.concatenate([k0, k0, k1, k1], axis=1)


def _head_gain(g, n_heads):
    return jnp.tile(g.astype(F32), n_heads)


def _by_residue(table, tm, dil):
    s = table.shape[0]
    return table.reshape(s // tm, tm // dil, dil, -1).transpose(0, 2, 1, 3).reshape(s, -1)


def kernel(x, norm1_g, w_in, qk_g, lam, subln_g, rpb, w_branch, w_out, norm2_g, w_up, conv_w, conv_b, w_down):
    b, s, d = x.shape
    depth = w_in.shape[0]
    t = b * s
    rows = s // GRID_W
    tm = min(TOKEN_TILE, s)
    t1d, tax = _rope_tables(s)
    t1d_dil = [tuple(_by_residue(tb, tm, dil) for tb in t1d) for _, dil in DIL_GROUPS]
    x2 = x.reshape(t, d)

    for l in range(depth):
        lambda_init = 0.8 - 0.6 * math.exp(-0.3 * l)
        w = w_in[l]
        w_r1 = jnp.concatenate([_cols(w, "aq"), _cols(w, "ak")], axis=1).astype(BF16)
        w_c = [jnp.concatenate([_cols(w, n, gi) for n in ("cq", "ck", "cv")], axis=1).astype(BF16)
               for gi in range(len(DIL_GROUPS))]
        w_r2 = jnp.concatenate([_cols(w, "bq"), _dup_heads(_cols(w, "bk"))], axis=1).astype(BF16)
        w_r3 = jnp.concatenate([_cols(w, "dq"), _cols(w, "dk")], axis=1).astype(BF16)
        w_pv = jnp.concatenate([_cols(w, "dv"), _cols(w, "av"), _dup_heads(_cols(w, "bv"))],
                               axis=1).astype(BF16)
        w_g = _cols(w, "g").astype(BF16)
        gq = qk_g[l]
        gain1 = jnp.concatenate([_head_gain(gq[0, 0], 8), _head_gain(gq[0, 1], 8)])[None, :]
        gain_c = jnp.concatenate([_head_gain(gq[2, 0], 8), _head_gain(gq[2, 1], 8),
                                  jnp.ones((BRANCH_W,), F32)])[None, :]
        gain2 = jnp.concatenate([_head_gain(gq[1, 0], 8), _head_gain(gq[1, 1], 4)])[None, :]
        gain3 = jnp.concatenate([_head_gain(gq[3, 0], 8), _head_gain(gq[3, 1], 8)])[None, :]
        g1 = norm1_g[l].astype(F32)[None, :]

        half = HEAD_DIM // 2
        r1 = _proj(x2, g1, w_r1, mode="rope", tm=tm, tn=512, seq=s, gain=gain1, tables=t1d,
                   shift=half).reshape(b, s, -1)
        qkv_c = [_proj(x2, g1, w_c[gi], mode="rope", tm=tm, tn=BRANCH_W, seq=s, gain=gain_c,
                       tables=t1d_dil[gi], shift=half, dil=dil, rope_blocks=2)
                 for gi, (_, dil) in enumerate(DIL_GROUPS)]
        r2 = _proj(x2, g1, w_r2, mode="rope", tm=tm, tn=768, seq=s, gain=gain2, tables=tax,
                   shift=half // 2).reshape(b, s, -1)
        r3 = _proj(x2, g1, w_r3, mode="norm", tm=tm, tn=512, seq=s, gain=gain3).reshape(b, s, -1)
        pv = _proj(x2, g1, w_pv, mode="plain", tm=tm, tn=1280, seq=s).reshape(b, s, -1)
        gate = _proj(x2, g1, w_g, mode="sigmoid", tm=tm, tn=512, seq=s)

        o_a = _diff_attention(r1, pv, lam[l].astype(F32), subln_g[l].astype(F32)[None, :],
                              lambda_init, tq=min(ATTN_Q_TILE, s))
        o_b = _gqa_attention(r2, pv, tq=min(ATTN_Q_TILE, s))
        ocs, lses = zip(*[_dilated_group(qkv) for qkv in qkv_c])
        o_d = _nbr_attention(r3, pv, _nbr_bias_table(rpb[l], rows))

        x2 = _merge(x2, o_a.reshape(t, BRANCH_W), o_b.reshape(t, BRANCH_W), ocs, lses,
                    o_d.reshape(t, BRANCH_W), gate, w_branch[l].astype(BF16), w_out[l].astype(BF16),
                    tm=min(MERGE_TILE, s), seq=s)
        x2 = _ffn(x2, norm2_g[l].astype(F32)[None, :], w_up[l].astype(BF16), conv_w[l].astype(F32),
                  conv_b[l].astype(F32)[None, :], w_down[l].astype(BF16), tm=tm, tn=FFN_COL_TILE, seq=s)
    return x2.reshape(b, s, d)
```

```python
import functools
import math

import jax
import jax.numpy as jnp
import numpy as np
from jax import lax
from jax.experimental import pallas as pl
from jax.experimental.pallas import tpu as pltpu

D_MODEL = 1024
HEAD_DIM = 64
LANES = 128
MXU_COLS = 256
ROPE_THETA = 10000.0
GRID_W = 64
EPS = 1e-6
NEG_INF = -1e30
SCALE = HEAD_DIM ** -0.5

DA_HEADS = 4
DA_HEADS_PER_STEP = 2
GQ_HEADS = 8
GQ_KV = 2
DIL_GROUPS = ((128, 1), (512, 4), (2048, 16))
DIL_HEADS = 8
DIL_RADIUS = 64
NA_HEADS = 8
NA_ROWS = 8
NA_COLS = 16
N_BRANCH = 4
BRANCH_W = 512
D_FF = 2816
CONV_HALO = 16
TOKEN_TILE = 1024
PROJ_ROW_CHUNK = 512
ATTN_Q_TILE = 256
MERGE_TILE = 512
FFN_COL_TILE = 256
BAND_SEGMENT = 1024
NBR_QROWS = 4
NBR_KROWS = NBR_QROWS + NA_ROWS

_OFF = {}
_o = 0
for _name, _n in (("aq", 512), ("ak", 512), ("av", 512), ("bq", 512), ("bk", 128), ("bv", 128),
                  ("cq", 1536), ("ck", 1536), ("cv", 1536), ("dq", 512), ("dk", 512), ("dv", 512),
                  ("g", 4096)):
    _OFF[_name] = (_o, _n)
    _o += _n

VMEM_LIMIT = 56 * 1024 * 1024

BF16 = jnp.bfloat16
F32 = jnp.float32


def _cparams(n_axes):
    return pltpu.CompilerParams(dimension_semantics=("arbitrary",) * n_axes,
                                vmem_limit_bytes=VMEM_LIMIT)


def _group_sumsq(y, bd):
    sq = y * y
    hi = sq.astype(BF16)
    lo = (sq - hi.astype(F32)).astype(BF16)
    return (jnp.dot(hi, bd, preferred_element_type=F32)
            + jnp.dot(lo, bd, preferred_element_type=F32))


def _proj_kernel(*refs, mode, tn, shift, dil, rope_blocks):
    xs_scr = None
    if dil > 1:
        refs, xs_scr = refs[:-1], refs[-1]
    if mode == "rope":
        x_ref, g1_ref, w_ref, gain_ref, bd_ref, c_ref, sa_ref, sb_ref, o_ref, h_scr = refs
    elif mode == "norm":
        x_ref, g1_ref, w_ref, gain_ref, bd_ref, o_ref, h_scr = refs
    else:
        x_ref, g1_ref, w_ref, o_ref, h_scr = refs
    j = pl.program_id(1)
    sub = x_ref.shape[0] // dil
    slabs = x_ref.shape[1] // LANES

    @pl.when(j == 0)
    def _():
        g1 = g1_ref[...]
        if dil > 1:
            for c in range(slabs):
                xs_scr[c] = x_ref[:, c * LANES:(c + 1) * LANES]
        for rho in range(dil):
            if dil == 1:
                x = x_ref[...]
            else:
                x = jnp.concatenate([xs_scr[c, pl.ds(rho, sub, stride=dil), :] for c in range(slabs)], axis=1)
            ms = jnp.mean(x * x, axis=-1, keepdims=True)
            h_scr[rho * sub:(rho + 1) * sub, :] = (x * lax.rsqrt(ms + EPS) * g1).astype(BF16)

    tm = x_ref.shape[0]
    chunk = min(PROJ_ROW_CHUNK, tm)
    w = w_ref[:, pl.ds(pl.multiple_of(j * tn, tn), tn)]

    def store(r0, val):
        val = val.astype(o_ref.dtype)
        if len(o_ref.shape) == 2:
            o_ref[r0:r0 + chunk, :] = val
        else:
            piece = min(chunk, sub)
            for p in range(r0, r0 + chunk, piece):
                o_ref[p // sub, p % sub:p % sub + piece, :] = val[p - r0:p - r0 + piece]

    def normed(r0, y):
        bd = bd_ref[...]
        parts = []
        for c in range(tn // MXU_COLS):
            yc = y[:, c * MXU_COLS:(c + 1) * MXU_COLS]
            ss = _group_sumsq(yc, bd)
            parts.append(yc * lax.rsqrt(ss * (1.0 / HEAD_DIM) + EPS))
        n = (parts[0] if len(parts) == 1 else jnp.concatenate(parts, axis=1)) * gain_ref[...]
        if mode == "rope":
            reps = tn // LANES
            cos = jnp.concatenate([c_ref[r0:r0 + chunk, :]] * reps, axis=1)
            sa = jnp.concatenate([sa_ref[r0:r0 + chunk, :]] * reps, axis=1)
            sb = jnp.concatenate([sb_ref[r0:r0 + chunk, :]] * reps, axis=1)
            n = (n * cos + pltpu.roll(n, tn - shift, 1) * sa + pltpu.roll(n, shift, 1) * sb)
        return n

    def sweep(epilogue):
        for r0 in range(0, tm, chunk):
            y = jnp.dot(h_scr[r0:r0 + chunk, :], w, preferred_element_type=F32)
            store(r0, epilogue(r0, y))

    plain = lambda r0, y: y
    if mode == "plain":
        sweep(plain)
    elif mode == "sigmoid":
        sweep(lambda r0, y: 0.5 * jnp.tanh(0.5 * y) + 0.5)
    elif rope_blocks is None:
        sweep(normed)
    else:
        pl.when(j < rope_blocks)(lambda: sweep(normed))
        pl.when(j >= rope_blocks)(lambda: sweep(plain))


def _proj(x2, g1, w, *, mode, tm, tn, seq, gain=None, tables=None, shift=0, dil=None, rope_blocks=None):
    t, d = x2.shape
    n = w.shape[1]
    per_seq = seq // tm
    assert t % tm == 0 and n % tn == 0 and seq % tm == 0
    in_specs = [pl.BlockSpec((tm, d), lambda i, j: (i, 0)),
                pl.BlockSpec((1, d), lambda i, j: (0, 0)),
                pl.BlockSpec((d, n), lambda i, j: (0, 0))]
    args = [x2, g1, w]
    if mode in ("rope", "norm"):
        assert tn % MXU_COLS == 0
        blk = np.kron(np.eye(MXU_COLS // HEAD_DIM), np.ones((HEAD_DIM, HEAD_DIM)))
        in_specs += [pl.BlockSpec((1, tn), lambda i, j: (0, j)),
                     pl.BlockSpec((MXU_COLS, MXU_COLS), lambda i, j: (0, 0))]
        args += [gain, jnp.asarray(blk, BF16)]
    if mode == "rope":
        for tb in tables:
            in_specs.append(pl.BlockSpec((tm, LANES), lambda i, j: (i % per_seq, 0)))
            args.append(tb)
    if dil is None:
        out_shape = jax.ShapeDtypeStruct((t, n), BF16)
        out_spec = pl.BlockSpec((tm, tn), lambda i, j: (i, j))
    else:
        assert tm % (dil * 16) == 0
        out_shape = jax.ShapeDtypeStruct((t // seq, dil, seq // dil, n), BF16)
        out_spec = pl.BlockSpec((None, dil, tm // dil, tn), lambda i, j: (i // per_seq, 0, i % per_seq, j))
    scratch = [pltpu.VMEM((tm, d), BF16)]
    if dil is not None and dil > 1:
        scratch.append(pltpu.VMEM((d // LANES, tm, LANES), F32))
    return pl.pallas_call(
        functools.partial(_proj_kernel, mode=mode, tn=tn, shift=shift, dil=dil or 1, rope_blocks=rope_blocks),
        out_shape=out_shape,
        grid=(t // tm, n // tn),
        in_specs=in_specs,
        out_specs=out_spec,
        scratch_shapes=scratch,
        compiler_params=_cparams(2),
        name=f"proj_{mode}{shift}" + (f"_d{dil}" if dil else ""),
    )(*args)


def _half_masks(rows):
    lane = lax.broadcasted_iota(jnp.int32, (rows, LANES), 1)
    return lane < HEAD_DIM


def _masked_q(q_ref_val, lo_mask):
    qf = q_ref_val.astype(F32) * SCALE
    q_lo = jnp.where(lo_mask, qf, 0.0).astype(BF16)
    q_hi = jnp.where(lo_mask, 0.0, qf).astype(BF16)
    return q_lo, q_hi


def _scores(q, k):
    return lax.dot_general(q, k, (((1,), (1,)), ((), ())), preferred_element_type=F32)


def _diff_attn_kernel(q_ref, k_ref, v_ref, lam_ref, sg_ref, o_ref, *, lambda_init):
    tq = q_ref.shape[0]
    lo = _half_masks(tq)
    lam = lam_ref[...]
    lam_val = (jnp.exp(jnp.sum(lam[0:1] * lam[1:2], axis=-1, keepdims=True))
               - jnp.exp(jnp.sum(lam[2:3] * lam[3:4], axis=-1, keepdims=True)) + lambda_init)
    for hh in range(q_ref.shape[1] // LANES):
        cols = slice(hh * LANES, (hh + 1) * LANES)
        q1, q2 = _masked_q(q_ref[:, cols], lo)
        k = k_ref[:, cols]
        s1 = _scores(q1, k)
        e1 = jnp.exp(s1 - jnp.max(s1, axis=-1, keepdims=True))
        r1 = 1.0 / jnp.sum(e1, axis=-1, keepdims=True)
        s2 = _scores(q2, k)
        e2 = jnp.exp(s2 - jnp.max(s2, axis=-1, keepdims=True))
        r2 = lam_val / jnp.sum(e2, axis=-1, keepdims=True)
        a = (e1 * r1 - e2 * r2).astype(BF16)
        o = jnp.dot(a, v_ref[:, cols], preferred_element_type=F32)
        ms = jnp.mean(o * o, axis=-1, keepdims=True)
        o_ref[:, cols] = (o * lax.rsqrt(ms + EPS) * sg_ref[...] * (1.0 - lambda_init)).astype(o_ref.dtype)


def _diff_attention(r1, pv, lam, subln_g, lambda_init, *, tq):
    b, s, _ = r1.shape
    hw = DA_HEADS_PER_STEP * LANES
    first_k = DA_HEADS // DA_HEADS_PER_STEP
    return pl.pallas_call(
        functools.partial(_diff_attn_kernel, lambda_init=lambda_init),
        out_shape=jax.ShapeDtypeStruct((b, s, BRANCH_W), BF16),
        grid=(b, DA_HEADS // DA_HEADS_PER_STEP, s // tq),
        in_specs=[pl.BlockSpec((None, tq, hw), lambda bi, h, i: (bi, i, h)),
                  pl.BlockSpec((None, s, hw), lambda bi, h, i: (bi, 0, first_k + h)),
                  pl.BlockSpec((None, s, hw), lambda bi, h, i: (bi, 0, first_k + h)),
                  pl.BlockSpec((4, HEAD_DIM), lambda bi, h, i: (0, 0)),
                  pl.BlockSpec((1, LANES), lambda bi, h, i: (0, 0))],
        out_specs=pl.BlockSpec((None, tq, hw), lambda bi, h, i: (bi, i, h)),
        compiler_params=_cparams(3),
        name="diff_attn",
    )(r1, r1, pv, lam, subln_g)


def _gqa_kernel(q_ref, k_ref, v_ref, o_ref):
    tq = q_ref.shape[0]
    lo = _half_masks(tq)
    k = k_ref[...]
    v = v_ref[...]
    for c in range(q_ref.shape[1] // LANES):
        outs = []
        for qh in _masked_q(q_ref[:, c * LANES:(c + 1) * LANES], lo):
            s = _scores(qh, k)
            e = jnp.exp(s - jnp.max(s, axis=-1, keepdims=True))
            r = 1.0 / jnp.sum(e, axis=-1, keepdims=True)
            outs.append(jnp.dot(e.astype(BF16), v, preferred_element_type=F32) * r)
        o_ref[:, c * LANES:(c + 1) * LANES] = jnp.where(lo, outs[0], outs[1]).astype(o_ref.dtype)


def _gqa_attention(r2, pv, *, tq):
    b, s, _ = r2.shape
    qw = GQ_HEADS // GQ_KV * HEAD_DIM
    return pl.pallas_call(
        _gqa_kernel,
        out_shape=jax.ShapeDtypeStruct((b, s, BRANCH_W), BF16),
        grid=(b, GQ_KV, s // tq),
        in_specs=[pl.BlockSpec((None, tq, qw), lambda bi, g, i: (bi, i, g)),
                  pl.BlockSpec((None, s, LANES), lambda bi, g, i: (bi, 0, 4 + g)),
                  pl.BlockSpec((None, s, LANES), lambda bi, g, i: (bi, 0, 8 + g))],
        out_specs=pl.BlockSpec((None, tq, qw), lambda bi, g, i: (bi, i, g)),
        compiler_params=_cparams(3),
        name="gqa_attn",
    )(r2, r2, pv)


def _band_kernel(q_ref, k_ref, v_ref, o_ref, lse_ref, *, tq, span):
    sub_len = k_ref.shape[0]
    seg_len = q_ref.shape[0]
    base = pl.program_id(2) * seg_len
    lo = _half_masks(tq)
    q_pos = lax.broadcasted_iota(jnp.int32, (tq, span), 0)
    k_pos = lax.broadcasted_iota(jnp.int32, (tq, span), 1)

    def block(ib, carry):
        i0 = pl.multiple_of(ib * tq, tq)
        k0 = pl.multiple_of(jnp.clip(base + i0 - DIL_RADIUS, 0, sub_len - span), DIL_RADIUS)
        ok = jnp.abs((k_pos + k0) - (q_pos + (base + i0))) <= DIL_RADIUS
        for hp in range(DIL_HEADS // 2):
            cols = slice(hp * LANES, (hp + 1) * LANES)
            k = k_ref[pl.ds(k0, span), cols]
            v = v_ref[pl.ds(k0, span), cols]
            outs, lses = [], []
            for qh in _masked_q(q_ref[pl.ds(i0, tq), cols], lo):
                s = jnp.where(ok, _scores(qh, k), NEG_INF)
                m = jnp.max(s, axis=-1, keepdims=True)
                e = jnp.exp(s - m)
                l = jnp.sum(e, axis=-1, keepdims=True)
                outs.append(jnp.dot(e.astype(BF16), v, preferred_element_type=F32) * (1.0 / l))
                lses.append(jnp.broadcast_to(m + jnp.log(l), (tq, LANES)))
            o_ref[pl.ds(i0, tq), cols] = jnp.where(lo, outs[0], outs[1]).astype(o_ref.dtype)
            lse_ref[pl.ds(i0, tq), cols] = jnp.where(lo, lses[0], lses[1])
        return carry

    lax.fori_loop(0, seg_len // tq, block, 0)


def _dilated_group(qkv):
    b, dil, sub_len, _ = qkv.shape
    tq = min(128, sub_len)
    span = min(tq + 2 * DIL_RADIUS, sub_len)
    seg = min(BAND_SEGMENT, sub_len)
    q_spec = pl.BlockSpec((None, None, seg, BRANCH_W), lambda bi, r, sg: (bi, r, sg, 0))
    return pl.pallas_call(
        functools.partial(_band_kernel, tq=tq, span=span),
        out_shape=(jax.ShapeDtypeStruct((b, dil, sub_len, BRANCH_W), BF16),
                   jax.ShapeDtypeStruct((b, dil, sub_len, BRANCH_W), F32)),
        grid=(b, dil, sub_len // seg),
        in_specs=[q_spec,
                  pl.BlockSpec((None, None, sub_len, BRANCH_W), lambda bi, r, sg: (bi, r, 0, 1)),
                  pl.BlockSpec((None, None, sub_len, BRANCH_W), lambda bi, r, sg: (bi, r, 0, 2))],
        out_specs=(q_spec, q_spec),
        compiler_params=_cparams(3),
        name=f"band_attn_d{dil}",
    )(qkv, qkv, qkv)


def _nbr_kernel(q_ref, k_ref, v_ref, bias_ref, o_ref, *, rows):
    ib = pl.program_id(1)
    kbase = jnp.clip(ib * NBR_QROWS - NA_ROWS // 2, 0, rows - NBR_KROWS)
    t0 = pl.multiple_of(kbase * GRID_W, GRID_W)
    nk = NBR_KROWS * GRID_W
    lo = _half_masks(NBR_QROWS * GRID_W)
    for hp in range(NA_HEADS // 2):
        cols = slice(hp * LANES, (hp + 1) * LANES)
        k = k_ref[pl.ds(t0, nk), cols]
        v = v_ref[pl.ds(t0, nk), cols]
        outs = []
        for half, qh in enumerate(_masked_q(q_ref[:, cols], lo)):
            s = _scores(qh, k) + bias_ref[2 * hp + half]
            e = jnp.exp(s - jnp.max(s, axis=-1, keepdims=True))
            r = 1.0 / jnp.sum(e, axis=-1, keepdims=True)
            outs.append(jnp.dot(e.astype(BF16), v, preferred_element_type=F32) * r)
        o_ref[:, cols] = jnp.where(lo, outs[0], outs[1]).astype(o_ref.dtype)


def _nbr_bias_table(rpb, rows):
    assert rows >= NBR_KROWS and rows % NBR_QROWS == 0
    c = np.arange(GRID_W)[:, None]
    kc = np.arange(GRID_W)[None, :]
    win = np.clip(c - NA_COLS // 2, 0, GRID_W - NA_COLS)
    col_ok = (kc >= win) & (kc < win + NA_COLS)
    col_idx = np.clip(kc - c + NA_COLS - 1, 0, 2 * NA_COLS - 2)
    rel = np.zeros((3, NBR_QROWS, NBR_KROWS), np.int32)
    row_ok = np.zeros((3, NBR_QROWS, NBR_KROWS), bool)
    for vi, i0 in enumerate((0, NBR_QROWS, rows - NBR_QROWS)):
        krow = np.clip(i0 - NA_ROWS // 2, 0, rows - NBR_KROWS) + np.arange(NBR_KROWS)
        for ri in range(NBR_QROWS):
            rs = np.clip(i0 + ri - NA_ROWS // 2, 0, rows - NA_ROWS)
            row_ok[vi, ri] = (krow >= rs) & (krow < rs + NA_ROWS)
            rel[vi, ri] = np.clip(krow - (i0 + ri) + NA_ROWS - 1, 0, 2 * NA_ROWS - 2)
    t = rpb.astype(F32)[:, rel][..., col_idx]
    ok = row_ok[:, :, :, None, None] & col_ok[None, None, None]
    t = jnp.where(ok[None], t, NEG_INF)
    t = jnp.transpose(t, (1, 0, 2, 4, 3, 5))
    return t.reshape(3, NA_HEADS, NBR_QROWS * GRID_W, NBR_KROWS * GRID_W)


def _nbr_attention(r3, pv, bias_tbl):
    b, s, _ = r3.shape
    rows = s // GRID_W
    nblk = rows // NBR_QROWS
    tq, nk = NBR_QROWS * GRID_W, NBR_KROWS * GRID_W

    def variant(ib):
        return jnp.where(ib == 0, 0, jnp.where(ib == nblk - 1, 2, 1))

    return pl.pallas_call(
        functools.partial(_nbr_kernel, rows=rows),
        out_shape=jax.ShapeDtypeStruct((b, s, BRANCH_W), BF16),
        grid=(b, nblk),
        in_specs=[pl.BlockSpec((None, tq, BRANCH_W), lambda bi, ib: (bi, ib, 0)),
                  pl.BlockSpec((None, s, BRANCH_W), lambda bi, ib: (bi, 0, 1)),
                  pl.BlockSpec((None, s, BRANCH_W), lambda bi, ib: (bi, 0, 0)),
                  pl.BlockSpec((None, NA_HEADS, tq, nk), lambda bi, ib: (variant(ib), 0, 0, 0))],
        out_specs=pl.BlockSpec((None, tq, BRANCH_W), lambda bi, ib: (bi, ib, 0)),
        compiler_params=_cparams(2),
        name="nbr_attn",
    )(r3, r3, pv, bias_tbl)


def _merge_kernel(x_ref, oa_ref, ob_ref, oc0_ref, oc1_ref, oc2_ref, l0_ref, l1_ref, l2_ref, od_ref,
                  gate_ref, wb_ref, wo_ref, o_ref, *scr):
    tm = x_ref.shape[0]

    def natural(ref, buf):
        dil = ref.shape[0]
        if dil == 1:
            return ref[0].astype(F32)
        slabs = ref.shape[2] // LANES
        for rho in range(dil):
            val = ref[rho].astype(F32)
            for c in range(slabs):
                buf[c, pl.ds(rho, tm // dil, stride=dil), :] = val[:, c * LANES:(c + 1) * LANES]
        return jnp.concatenate([buf[c] for c in range(slabs)], axis=1)

    l0 = natural(l0_ref, None)
    l1, l2 = natural(l1_ref, scr[0]), natural(l2_ref, scr[1])
    m = jnp.maximum(jnp.maximum(l0, l1), l2)
    w0, w1, w2 = jnp.exp(l0 - m), jnp.exp(l1 - m), jnp.exp(l2 - m)
    oc = (w0 * natural(oc0_ref, None) + w1 * natural(oc1_ref, scr[2])
          + w2 * natural(oc2_ref, scr[3])) * (1.0 / (w0 + w1 + w2))
    branches = (oa_ref[...], ob_ref[...], oc.astype(BF16), od_ref[...])
    merged = None
    for bi, ob in enumerate(branches):
        t = (jnp.dot(ob, wb_ref[bi], preferred_element_type=F32)
             * gate_ref[:, bi * D_MODEL:(bi + 1) * D_MODEL].astype(F32))
        merged = t if merged is None else merged + t
    o_ref[...] = x_ref[...] + jnp.dot(merged.astype(BF16), wo_ref[...], preferred_element_type=F32)


def _merge(x2, oa, ob, ocs, lses, od, gate, wb, wo, *, tm, seq):
    t, d = x2.shape
    per_seq = seq // tm
    row = lambda w: pl.BlockSpec((tm, w), lambda i: (i, 0))

    def classes(a):
        dil = a.shape[1]
        assert tm % (dil * 16) == 0
        return pl.BlockSpec((None, dil, tm // dil, BRANCH_W), lambda i: (i // per_seq, 0, i % per_seq, 0))

    return pl.pallas_call(
        _merge_kernel,
        out_shape=jax.ShapeDtypeStruct((t, d), F32),
        grid=(t // tm,),
        in_specs=([row(d), row(BRANCH_W), row(BRANCH_W)] + [classes(a) for a in ocs]
                  + [classes(a) for a in lses]
                  + [row(BRANCH_W), row(N_BRANCH * d),
                     pl.BlockSpec((N_BRANCH, BRANCH_W, d), lambda i: (0, 0, 0)),
                     pl.BlockSpec((d, d), lambda i: (0, 0))]),
        out_specs=row(d),
        scratch_shapes=[pltpu.VMEM((BRANCH_W // LANES, tm, LANES), F32)] * 4,
        compiler_params=_cparams(1),
        name="merge",
    )(x2, oa, ob, *ocs, *lses, od, gate, wb, wo)


def _ffn_kernel(x_ref, xp_ref, xn_ref, g2_ref, wa_ref, wg_ref, cwa_ref, cwg_ref, cba_ref, cbg_ref,
                wd_ref, o_ref, h_scr, acc_scr, *, tiles_per_seq):
    i, j = pl.program_id(0), pl.program_id(1)
    tm = x_ref.shape[0]
    ext = tm + 2 * CONV_HALO

    @pl.when(j == 0)
    def _():
        g2 = g2_ref[...]

        def norm(x):
            return x * lax.rsqrt(jnp.mean(x * x, axis=-1, keepdims=True) + EPS) * g2

        first = (i % tiles_per_seq) == 0
        last = (i % tiles_per_seq) == tiles_per_seq - 1
        h_scr[0:CONV_HALO, :] = jnp.where(first, 0.0, norm(xp_ref[...])).astype(BF16)
        h_scr[CONV_HALO:CONV_HALO + tm, :] = norm(x_ref[...]).astype(BF16)
        h_scr[CONV_HALO + tm:ext, :] = jnp.where(last, 0.0, norm(xn_ref[...])).astype(BF16)
        acc_scr[...] = jnp.zeros_like(acc_scr)

    h = h_scr[...]

    def conv(w_ref, cw_ref, cb_ref):
        u = jnp.dot(h, w_ref[...], preferred_element_type=F32)
        prev = pltpu.roll(u, 1, 0)[CONV_HALO:CONV_HALO + tm]
        nxt = pltpu.roll(u, ext - 1, 0)[CONV_HALO:CONV_HALO + tm]
        cw = cw_ref[...]
        return (cw[0:1] * prev + cw[1:2] * u[CONV_HALO:CONV_HALO + tm] + cw[2:3] * nxt
                + cb_ref[...])

    a = conv(wa_ref, cwa_ref, cba_ref)
    g = conv(wg_ref, cwg_ref, cbg_ref)
    hm = (a * (1.0 / (1.0 + jnp.exp(-a))) * g).astype(BF16)
    acc_scr[...] += jnp.dot(hm, wd_ref[...], preferred_element_type=F32)

    @pl.when(j == pl.num_programs(1) - 1)
    def _():
        o_ref[...] = x_ref[...] + acc_scr[...]


def _ffn(x2, g2, w_up, conv_w, conv_b, w_down, *, tm, tn, seq):
    t, d = x2.shape
    nj = D_FF // tn
    assert D_FF % tn == 0 and seq % tm == 0 and tm % CONV_HALO == 0
    hb = tm // CONV_HALO
    n_halo = t // CONV_HALO
    return pl.pallas_call(
        functools.partial(_ffn_kernel, tiles_per_seq=seq // tm),
        out_shape=jax.ShapeDtypeStruct((t, d), F32),
        grid=(t // tm, nj),
        in_specs=[pl.BlockSpec((tm, d), lambda i, j: (i, 0)),
                  pl.BlockSpec((CONV_HALO, d), lambda i, j: (jnp.maximum(i * hb - 1, 0), 0)),
                  pl.BlockSpec((CONV_HALO, d), lambda i, j: (jnp.minimum((i + 1) * hb, n_halo - 1), 0)),
                  pl.BlockSpec((1, d), lambda i, j: (0, 0)),
                  pl.BlockSpec((d, tn), lambda i, j: (0, j)),
                  pl.BlockSpec((d, tn), lambda i, j: (0, nj + j)),
                  pl.BlockSpec((3, tn), lambda i, j: (0, j)),
                  pl.BlockSpec((3, tn), lambda i, j: (0, nj + j)),
                  pl.BlockSpec((1, tn), lambda i, j: (0, j)),
                  pl.BlockSpec((1, tn), lambda i, j: (0, nj + j)),
                  pl.BlockSpec((tn, d), lambda i, j: (j, 0))],
        out_specs=pl.BlockSpec((tm, d), lambda i, j: (i, 0)),
        scratch_shapes=[pltpu.VMEM((tm + 2 * CONV_HALO, d), BF16), pltpu.VMEM((tm, d), F32)],
        compiler_params=_cparams(2),
        name="conv_ffn",
    )(x2, x2, x2, g2, w_up, w_up, conv_w, conv_w, conv_b, conv_b, w_down)


def _rope_tables(seq):
    pos = jnp.arange(seq, dtype=jnp.int32)

    def angles(p, dim):
        inv = ROPE_THETA ** (-(jnp.arange(0, dim, 2, dtype=F32) / dim))
        return p.astype(F32)[:, None] * inv[None, :]

    def tables(ang_per_lane, first_half):
        c, s = jnp.cos(ang_per_lane), jnp.sin(ang_per_lane)
        sa = jnp.where(first_half[None, :], -s, 0.0)
        sb = jnp.where(first_half[None, :], 0.0, s)
        return tuple(jnp.tile(t, (1, LANES // HEAD_DIM)) for t in (c, sa, sb))

    lane = np.arange(HEAD_DIM)
    half = HEAD_DIM // 2
    a1 = angles(pos, HEAD_DIM)
    t1 = tables(jnp.concatenate([a1, a1], axis=1), lane < half)
    ar = angles(pos // GRID_W, half)
    ac = angles(pos % GRID_W, half)
    t2 = tables(jnp.concatenate([ar, ar, ac, ac], axis=1), (lane % half) < half // 2)
    return t1, t2


def _cols(w, name, part=None):
    o, n = _OFF[name]
    if part is not None:
        n //= len(DIL_GROUPS)
        o += part * n
    return w[:, o:o + n]


def _dup_heads(w):
    k0, k1 = w[:, :HEAD_DIM], w[:, HEAD_DIM:]
    return jnp.concatenate([k0, k0, k1, k1], axis=1)


def _head_gain(g, n_heads):
    return jnp.tile(g.astype(F32), n_heads)


def _by_residue(table, tm, dil):
    s = table.shape[0]
    return table.reshape(s // tm, tm // dil, dil, -1).transpose(0, 2, 1, 3).reshape(s, -1)


def kernel(x, norm1_g, w_in, qk_g, lam, subln_g, rpb, w_branch, w_out, norm2_g, w_up, conv_w, conv_b, w_down):
    b, s, d = x.shape
    depth = w_in.shape[0]
    t = b * s
    rows = s // GRID_W
    tm = min(TOKEN_TILE, s)
    t1d, tax = _rope_tables(s)
    t1d_dil = [tuple(_by_residue(tb, tm, dil) for tb in t1d) for _, dil in DIL_GROUPS]
    x2 = x.reshape(t, d)

    for l in range(depth):
        lambda_init = 0.8 - 0.6 * math.exp(-0.3 * l)
        w = w_in[l]
        w_r1 = jnp.concatenate([_cols(w, "aq"), _cols(w, "ak")], axis=1).astype(BF16)
        w_c = [jnp.concatenate([_cols(w, n, gi) for n in ("cq", "ck", "cv")], axis=1).astype(BF16)
               for gi in range(len(DIL_GROUPS))]
        w_r2 = jnp.concatenate([_cols(w, "bq"), _dup_heads(_cols(w, "bk"))], axis=1).astype(BF16)
        w_r3 = jnp.concatenate([_cols(w, "dq"), _cols(w, "dk")], axis=1).astype(BF16)
        w_pv = jnp.concatenate([_cols(w, "dv"), _cols(w, "av"), _dup_heads(_cols(w, "bv"))],
                               axis=1).astype(BF16)
        w_g = _cols(w, "g").astype(BF16)
        gq = qk_g[l]
        gain1 = jnp.concatenate([_head_gain(gq[0, 0], 8), _head_gain(gq[0, 1], 8)])[None, :]
        gain_c = jnp.concatenate([_head_gain(gq[2, 0], 8), _head_gain(gq[2, 1], 8),
                                  jnp.ones((BRANCH_W,), F32)])[None, :]
        gain2 = jnp.concatenate([_head_gain(gq[1, 0], 8), _head_gain(gq[1, 1], 4)])[None, :]
        gain3 = jnp.concatenate([_head_gain(gq[3, 0], 8), _head_gain(gq[3, 1], 8)])[None, :]
        g1 = norm1_g[l].astype(F32)[None, :]

        half = HEAD_DIM // 2
        r1 = _proj(x2, g1, w_r1, mode="rope", tm=tm, tn=512, seq=s, gain=gain1, tables=t1d,
                   shift=half).reshape(b, s, -1)
        qkv_c = [_proj(x2, g1, w_c[gi], mode="rope", tm=tm, tn=BRANCH_W, seq=s, gain=gain_c,
                       tables=t1d_dil[gi], shift=half, dil=dil, rope_blocks=2)
                 for gi, (_, dil) in enumerate(DIL_GROUPS)]
        r2 = _proj(x2, g1, w_r2, mode="rope", tm=tm, tn=768, seq=s, gain=gain2, tables=tax,
                   shift=half // 2).reshape(b, s, -1)
        r3 = _proj(x2, g1, w_r3, mode="norm", tm=tm, tn=512, seq=s, gain=gain3).reshape(b, s, -1)
        pv = _proj(x2, g1, w_pv, mode="plain", tm=tm, tn=1280, seq=s).reshape(b, s, -1)
        gate = _proj(x2, g1, w_g, mode="sigmoid", tm=tm, tn=1024, seq=s)

        o_a = _diff_attention(r1, pv, lam[l].astype(F32), subln_g[l].astype(F32)[None, :],
                              lambda_init, tq=min(ATTN_Q_TILE, s))
        o_b = _gqa_attention(r2, pv, tq=min(ATTN_Q_TILE, s))
        ocs, lses = zip(*[_dilated_group(qkv) for qkv in qkv_c])
        o_d = _nbr_attention(r3, pv, _nbr_bias_table(rpb[l], rows))

        x2 = _merge(x2, o_a.reshape(t, BRANCH_W), o_b.reshape(t, BRANCH_W), ocs, lses,
                    o_d.reshape(t, BRANCH_W), gate, w_branch[l].astype(BF16), w_out[l].astype(BF16),
                    tm=min(MERGE_TILE, s), seq=s)
        x2 = _ffn(x2, norm2_g[l].astype(F32)[None, :], w_up[l].astype(BF16), conv_w[l].astype(F32),
                  conv_b[l].astype(F32)[None, :], w_down[l].astype(BF16), tm=tm, tn=FFN_COL_TILE, seq=s)
    return x2.reshape(b, s, d)
```

```python
import functools
import math

import jax
import jax.numpy as jnp
import numpy as np
from jax import lax
from jax.experimental import pallas as pl
from jax.experimental.pallas import tpu as pltpu

D_MODEL = 1024
HEAD_DIM = 64
LANES = 128
MXU_COLS = 256
ROPE_THETA = 10000.0
GRID_W = 64
EPS = 1e-6
NEG_INF = -1e30
SCALE = HEAD_DIM ** -0.5

DA_HEADS = 4
DA_HEADS_PER_STEP = 2
GQ_HEADS = 8
GQ_KV = 2
DIL_GROUPS = ((128, 1), (512, 4), (2048, 16))
DIL_HEADS = 8
DIL_RADIUS = 64
NA_HEADS = 8
NA_ROWS = 8
NA_COLS = 16
N_BRANCH = 4
BRANCH_W = 512
D_FF = 2816
CONV_HALO = 16
TOKEN_TILE = 1024
PROJ_ROW_CHUNK = 512
ATTN_Q_TILE = 256
MERGE_TILE = 512
FFN_COL_TILE = 256
BAND_SEGMENT = 1024
NBR_QROWS = 4
NBR_KROWS = NBR_QROWS + NA_ROWS

_OFF = {}
_o = 0
for _name, _n in (("aq", 512), ("ak", 512), ("av", 512), ("bq", 512), ("bk", 128), ("bv", 128),
                  ("cq", 1536), ("ck", 1536), ("cv", 1536), ("dq", 512), ("dk", 512), ("dv", 512),
                  ("g", 4096)):
    _OFF[_name] = (_o, _n)
    _o += _n

VMEM_LIMIT = 56 * 1024 * 1024

BF16 = jnp.bfloat16
F32 = jnp.float32


def _cparams(n_axes):
    return pltpu.CompilerParams(dimension_semantics=("arbitrary",) * n_axes,
                                vmem_limit_bytes=VMEM_LIMIT)


def _group_sumsq(y, bd):
    sq = y * y
    hi = sq.astype(BF16)
    lo = (sq - hi.astype(F32)).astype(BF16)
    return (jnp.dot(hi, bd, preferred_element_type=F32)
            + jnp.dot(lo, bd, preferred_element_type=F32))


def _proj_kernel(*refs, mode, tn, shift, dil, rope_cols):
    xs_scr = None
    if dil > 1:
        refs, xs_scr = refs[:-1], refs[-1]
    if mode == "rope":
        x_ref, g1_ref, w_ref, gain_ref, bd_ref, c_ref, sa_ref, sb_ref, o_ref, h_scr = refs
    elif mode == "norm":
        x_ref, g1_ref, w_ref, gain_ref, bd_ref, o_ref, h_scr = refs
    else:
        x_ref, g1_ref, w_ref, o_ref, h_scr = refs
    j = pl.program_id(1)
    sub = x_ref.shape[0] // dil
    slabs = x_ref.shape[1] // LANES

    @pl.when(j == 0)
    def _():
        g1 = g1_ref[...]
        if dil > 1:
            for c in range(slabs):
                xs_scr[c] = x_ref[:, c * LANES:(c + 1) * LANES]
        for rho in range(dil):
            if dil == 1:
                x = x_ref[...]
            else:
                x = jnp.concatenate([xs_scr[c, pl.ds(rho, sub, stride=dil), :] for c in range(slabs)], axis=1)
            ms = jnp.mean(x * x, axis=-1, keepdims=True)
            h_scr[rho * sub:(rho + 1) * sub, :] = (x * lax.rsqrt(ms + EPS) * g1).astype(BF16)

    tm = x_ref.shape[0]
    chunk = min(PROJ_ROW_CHUNK, tm)
    w = w_ref[:, pl.ds(pl.multiple_of(j * tn, tn), tn)]

    def store(r0, val):
        val = val.astype(o_ref.dtype)
        if len(o_ref.shape) == 2:
            o_ref[r0:r0 + chunk, :] = val
        else:
            piece = min(chunk, sub)
            for p in range(r0, r0 + chunk, piece):
                o_ref[p // sub, p % sub:p % sub + piece, :] = val[p - r0:p - r0 + piece]

    def normed(r0, y_all):
        nw = tn if rope_cols is None else rope_cols
        y = y_all[:, :nw]
        bd = bd_ref[...]
        parts = []
        for c in range(nw // MXU_COLS):
            yc = y[:, c * MXU_COLS:(c + 1) * MXU_COLS]
            ss = _group_sumsq(yc, bd)
            parts.append(yc * lax.rsqrt(ss * (1.0 / HEAD_DIM) + EPS))
        n = (parts[0] if len(parts) == 1 else jnp.concatenate(parts, axis=1)) * gain_ref[:, :nw]
        if mode == "rope":
            reps = nw // LANES
            cos = jnp.concatenate([c_ref[r0:r0 + chunk, :]] * reps, axis=1)
            sa = jnp.concatenate([sa_ref[r0:r0 + chunk, :]] * reps, axis=1)
            sb = jnp.concatenate([sb_ref[r0:r0 + chunk, :]] * reps, axis=1)
            n = (n * cos + pltpu.roll(n, nw - shift, 1) * sa + pltpu.roll(n, shift, 1) * sb)
        return n if nw == tn else jnp.concatenate([n, y_all[:, nw:]], axis=1)

    def sweep(epilogue):
        for r0 in range(0, tm, chunk):
            y = jnp.dot(h_scr[r0:r0 + chunk, :], w, preferred_element_type=F32)
            store(r0, epilogue(r0, y))

    plain = lambda r0, y: y
    if mode == "plain":
        sweep(plain)
    elif mode == "sigmoid":
        sweep(lambda r0, y: 0.5 * jnp.tanh(0.5 * y) + 0.5)
    else:
        sweep(normed)


def _proj(x2, g1, w, *, mode, tm, tn, seq, gain=None, tables=None, shift=0, dil=None, rope_cols=None):
    assert rope_cols is None or (tn == w.shape[1] and rope_cols % MXU_COLS == 0)
    t, d = x2.shape
    n = w.shape[1]
    per_seq = seq // tm
    assert t % tm == 0 and n % tn == 0 and seq % tm == 0
    in_specs = [pl.BlockSpec((tm, d), lambda i, j: (i, 0)),
                pl.BlockSpec((1, d), lambda i, j: (0, 0)),
                pl.BlockSpec((d, n), lambda i, j: (0, 0))]
    args = [x2, g1, w]
    if mode in ("rope", "norm"):
        assert tn % MXU_COLS == 0
        blk = np.kron(np.eye(MXU_COLS // HEAD_DIM), np.ones((HEAD_DIM, HEAD_DIM)))
        in_specs += [pl.BlockSpec((1, tn), lambda i, j: (0, j)),
                     pl.BlockSpec((MXU_COLS, MXU_COLS), lambda i, j: (0, 0))]
        args += [gain, jnp.asarray(blk, BF16)]
    if mode == "rope":
        for tb in tables:
            in_specs.append(pl.BlockSpec((tm, LANES), lambda i, j: (i % per_seq, 0)))
            args.append(tb)
    if dil is None:
        out_shape = jax.ShapeDtypeStruct((t, n), BF16)
        out_spec = pl.BlockSpec((tm, tn), lambda i, j: (i, j))
    else:
        assert tm % (dil * 16) == 0
        out_shape = jax.ShapeDtypeStruct((t // seq, dil, seq // dil, n), BF16)
        out_spec = pl.BlockSpec((None, dil, tm // dil, tn), lambda i, j: (i // per_seq, 0, i % per_seq, j))
    scratch = [pltpu.VMEM((tm, d), BF16)]
    if dil is not None and dil > 1:
        scratch.append(pltpu.VMEM((d // LANES, tm, LANES), F32))
    return pl.pallas_call(
        functools.partial(_proj_kernel, mode=mode, tn=tn, shift=shift, dil=dil or 1, rope_cols=rope_cols),
        out_shape=out_shape,
        grid=(t // tm, n // tn),
        in_specs=in_specs,
        out_specs=out_spec,
        scratch_shapes=scratch,
        compiler_params=_cparams(2),
        name=f"proj_{mode}{shift}" + (f"_d{dil}" if dil else ""),
    )(*args)


def _half_masks(rows):
    lane = lax.broadcasted_iota(jnp.int32, (rows, LANES), 1)
    return lane < HEAD_DIM


def _masked_q(q_ref_val, lo_mask):
    qf = q_ref_val.astype(F32) * SCALE
    q_lo = jnp.where(lo_mask, qf, 0.0).astype(BF16)
    q_hi = jnp.where(lo_mask, 0.0, qf).astype(BF16)
    return q_lo, q_hi


def _scores(q, k):
    return lax.dot_general(q, k, (((1,), (1,)), ((), ())), preferred_element_type=F32)


def _diff_attn_kernel(q_ref, k_ref, v_ref, lam_ref, sg_ref, o_ref, *, lambda_init):
    tq = q_ref.shape[0]
    lo = _half_masks(tq)
    lam = lam_ref[...]
    lam_val = (jnp.exp(jnp.sum(lam[0:1] * lam[1:2], axis=-1, keepdims=True))
               - jnp.exp(jnp.sum(lam[2:3] * lam[3:4], axis=-1, keepdims=True)) + lambda_init)
    for hh in range(q_ref.shape[1] // LANES):
        cols = slice(hh * LANES, (hh + 1) * LANES)
        q1, q2 = _masked_q(q_ref[:, cols], lo)
        k = k_ref[:, cols]
        s1 = _scores(q1, k)
        e1 = jnp.exp(s1 - jnp.max(s1, axis=-1, keepdims=True))
        r1 = 1.0 / jnp.sum(e1, axis=-1, keepdims=True)
        s2 = _scores(q2, k)
        e2 = jnp.exp(s2 - jnp.max(s2, axis=-1, keepdims=True))
        r2 = lam_val / jnp.sum(e2, axis=-1, keepdims=True)
        a = (e1 * r1 - e2 * r2).astype(BF16)
        o = jnp.dot(a, v_ref[:, cols], preferred_element_type=F32)
        ms = jnp.mean(o * o, axis=-1, keepdims=True)
        o_ref[:, cols] = (o * lax.rsqrt(ms + EPS) * sg_ref[...] * (1.0 - lambda_init)).astype(o_ref.dtype)


def _diff_attention(r1, pv, lam, subln_g, lambda_init, *, tq):
    b, s, _ = r1.shape
    hw = DA_HEADS_PER_STEP * LANES
    first_k = DA_HEADS // DA_HEADS_PER_STEP
    return pl.pallas_call(
        functools.partial(_diff_attn_kernel, lambda_init=lambda_init),
        out_shape=jax.ShapeDtypeStruct((b, s, BRANCH_W), BF16),
        grid=(b, DA_HEADS // DA_HEADS_PER_STEP, s // tq),
        in_specs=[pl.BlockSpec((None, tq, hw), lambda bi, h, i: (bi, i, h)),
                  pl.BlockSpec((None, s, hw), lambda bi, h, i: (bi, 0, first_k + h)),
                  pl.BlockSpec((None, s, hw), lambda bi, h, i: (bi, 0, first_k + h)),
                  pl.BlockSpec((4, HEAD_DIM), lambda bi, h, i: (0, 0)),
                  pl.BlockSpec((1, LANES), lambda bi, h, i: (0, 0))],
        out_specs=pl.BlockSpec((None, tq, hw), lambda bi, h, i: (bi, i, h)),
        compiler_params=_cparams(3),
        name="diff_attn",
    )(r1, r1, pv, lam, subln_g)


def _gqa_kernel(q_ref, k_ref, v_ref, o_ref):
    tq = q_ref.shape[0]
    lo = _half_masks(tq)
    k = k_ref[...]
    v = v_ref[...]
    for c in range(q_ref.shape[1] // LANES):
        outs = []
        for qh in _masked_q(q_ref[:, c * LANES:(c + 1) * LANES], lo):
            s = _scores(qh, k)
            e = jnp.exp(s - jnp.max(s, axis=-1, keepdims=True))
            r = 1.0 / jnp.sum(e, axis=-1, keepdims=True)
            outs.append(jnp.dot(e.astype(BF16), v, preferred_element_type=F32) * r)
        o_ref[:, c * LANES:(c + 1) * LANES] = jnp.where(lo, outs[0], outs[1]).astype(o_ref.dtype)


def _gqa_attention(r2, pv, *, tq):
    b, s, _ = r2.shape
    qw = GQ_HEADS // GQ_KV * HEAD_DIM
    return pl.pallas_call(
        _gqa_kernel,
        out_shape=jax.ShapeDtypeStruct((b, s, BRANCH_W), BF16),
        grid=(b, GQ_KV, s // tq),
        in_specs=[pl.BlockSpec((None, tq, qw), lambda bi, g, i: (bi, i, g)),
                  pl.BlockSpec((None, s, LANES), lambda bi, g, i: (bi, 0, 4 + g)),
                  pl.BlockSpec((None, s, LANES), lambda bi, g, i: (bi, 0, 8 + g))],
        out_specs=pl.BlockSpec((None, tq, qw), lambda bi, g, i: (bi, i, g)),
        compiler_params=_cparams(3),
        name="gqa_attn",
    )(r2, r2, pv)


def _band_kernel(q_ref, k_ref, v_ref, o_ref, lse_ref, *, tq, span):
    sub_len = k_ref.shape[0]
    seg_len = q_ref.shape[0]
    base = pl.program_id(2) * seg_len
    lo = _half_masks(tq)
    q_pos = lax.broadcasted_iota(jnp.int32, (tq, span), 0)
    k_pos = lax.broadcasted_iota(jnp.int32, (tq, span), 1)

    def block(ib, carry):
        i0 = pl.multiple_of(ib * tq, tq)
        k0 = pl.multiple_of(jnp.clip(base + i0 - DIL_RADIUS, 0, sub_len - span), DIL_RADIUS)
        ok = jnp.abs((k_pos + k0) - (q_pos + (base + i0))) <= DIL_RADIUS
        for hp in range(DIL_HEADS // 2):
            cols = slice(hp * LANES, (hp + 1) * LANES)
            k = k_ref[pl.ds(k0, span), cols]
            v = v_ref[pl.ds(k0, span), cols]
            outs, lses = [], []
            for qh in _masked_q(q_ref[pl.ds(i0, tq), cols], lo):
                s = jnp.where(ok, _scores(qh, k), NEG_INF)
                m = jnp.max(s, axis=-1, keepdims=True)
                e = jnp.exp(s - m)
                l = jnp.sum(e, axis=-1, keepdims=True)
                outs.append(jnp.dot(e.astype(BF16), v, preferred_element_type=F32) * (1.0 / l))
                lses.append(jnp.broadcast_to(m + jnp.log(l), (tq, LANES)))
            o_ref[pl.ds(i0, tq), cols] = jnp.where(lo, outs[0], outs[1]).astype(o_ref.dtype)
            lse_ref[pl.ds(i0, tq), cols] = jnp.where(lo, lses[0], lses[1])
        return carry

    lax.fori_loop(0, seg_len // tq, block, 0)


def _dilated_group(qkv):
    b, dil, sub_len, _ = qkv.shape
    tq = min(128, sub_len)
    span = min(tq + 2 * DIL_RADIUS, sub_len)
    seg = min(BAND_SEGMENT, sub_len)
    q_spec = pl.BlockSpec((None, None, seg, BRANCH_W), lambda bi, r, sg: (bi, r, sg, 0))
    return pl.pallas_call(
        functools.partial(_band_kernel, tq=tq, span=span),
        out_shape=(jax.ShapeDtypeStruct((b, dil, sub_len, BRANCH_W), BF16),
                   jax.ShapeDtypeStruct((b, dil, sub_len, BRANCH_W), F32)),
        grid=(b, dil, sub_len // seg),
        in_specs=[q_spec,
                  pl.BlockSpec((None, None, sub_len, BRANCH_W), lambda bi, r, sg: (bi, r, 0, 1)),
                  pl.BlockSpec((None, None, sub_len, BRANCH_W), lambda bi, r, sg: (bi, r, 0, 2))],
        out_specs=(q_spec, q_spec),
        compiler_params=_cparams(3),
        name=f"band_attn_d{dil}",
    )(qkv, qkv, qkv)


def _nbr_kernel(q_ref, k_ref, v_ref, bias_ref, o_ref, *, rows):
    ib = pl.program_id(1)
    kbase = jnp.clip(ib * NBR_QROWS - NA_ROWS // 2, 0, rows - NBR_KROWS)
    t0 = pl.multiple_of(kbase * GRID_W, GRID_W)
    nk = NBR_KROWS * GRID_W
    lo = _half_masks(NBR_QROWS * GRID_W)
    for hp in range(NA_HEADS // 2):
        cols = slice(hp * LANES, (hp + 1) * LANES)
        k = k_ref[pl.ds(t0, nk), cols]
        v = v_ref[pl.ds(t0, nk), cols]
        outs = []
        for half, qh in enumerate(_masked_q(q_ref[:, cols], lo)):
            s = _scores(qh, k) + bias_ref[2 * hp + half]
            e = jnp.exp(s - jnp.max(s, axis=-1, keepdims=True))
            r = 1.0 / jnp.sum(e, axis=-1, keepdims=True)
            outs.append(jnp.dot(e.astype(BF16), v, preferred_element_type=F32) * r)
        o_ref[:, cols] = jnp.where(lo, outs[0], outs[1]).astype(o_ref.dtype)


def _nbr_bias_table(rpb, rows):
    assert rows >= NBR_KROWS and rows % NBR_QROWS == 0
    c = np.arange(GRID_W)[:, None]
    kc = np.arange(GRID_W)[None, :]
    win = np.clip(c - NA_COLS // 2, 0, GRID_W - NA_COLS)
    col_ok = (kc >= win) & (kc < win + NA_COLS)
    col_idx = np.clip(kc - c + NA_COLS - 1, 0, 2 * NA_COLS - 2)
    rel = np.zeros((3, NBR_QROWS, NBR_KROWS), np.int32)
    row_ok = np.zeros((3, NBR_QROWS, NBR_KROWS), bool)
    for vi, i0 in enumerate((0, NBR_QROWS, rows - NBR_QROWS)):
        krow = np.clip(i0 - NA_ROWS // 2, 0, rows - NBR_KROWS) + np.arange(NBR_KROWS)
        for ri in range(NBR_QROWS):
            rs = np.clip(i0 + ri - NA_ROWS // 2, 0, rows - NA_ROWS)
            row_ok[vi, ri] = (krow >= rs) & (krow < rs + NA_ROWS)
            rel[vi, ri] = np.clip(krow - (i0 + ri) + NA_ROWS - 1, 0, 2 * NA_ROWS - 2)
    t = rpb.astype(F32)[:, rel][..., col_idx]
    ok = row_ok[:, :, :, None, None] & col_ok[None, None, None]
    t = jnp.where(ok[None], t, NEG_INF)
    t = jnp.transpose(t, (1, 0, 2, 4, 3, 5))
    return t.reshape(3, NA_HEADS, NBR_QROWS * GRID_W, NBR_KROWS * GRID_W)


def _nbr_attention(r3, pv, bias_tbl):
    b, s, _ = r3.shape
    rows = s // GRID_W
    nblk = rows // NBR_QROWS
    tq, nk = NBR_QROWS * GRID_W, NBR_KROWS * GRID_W

    def variant(ib):
        return jnp.where(ib == 0, 0, jnp.where(ib == nblk - 1, 2, 1))

    return pl.pallas_call(
        functools.partial(_nbr_kernel, rows=rows),
        out_shape=jax.ShapeDtypeStruct((b, s, BRANCH_W), BF16),
        grid=(b, nblk),
        in_specs=[pl.BlockSpec((None, tq, BRANCH_W), lambda bi, ib: (bi, ib, 0)),
                  pl.BlockSpec((None, s, BRANCH_W), lambda bi, ib: (bi, 0, 1)),
                  pl.BlockSpec((None, s, BRANCH_W), lambda bi, ib: (bi, 0, 0)),
                  pl.BlockSpec((None, NA_HEADS, tq, nk), lambda bi, ib: (variant(ib), 0, 0, 0))],
        out_specs=pl.BlockSpec((None, tq, BRANCH_W), lambda bi, ib: (bi, ib, 0)),
        compiler_params=_cparams(2),
        name="nbr_attn",
    )(r3, r3, pv, bias_tbl)


def _merge_kernel(x_ref, oa_ref, ob_ref, oc0_ref, oc1_ref, oc2_ref, l0_ref, l1_ref, l2_ref, od_ref,
                  gate_ref, wb_ref, wo_ref, o_ref, *scr):
    tm = x_ref.shape[0]

    def natural(ref, buf):
        dil = ref.shape[0]
        if dil == 1:
            return ref[0].astype(F32)
        slabs = ref.shape[2] // LANES
        for rho in range(dil):
            val = ref[rho].astype(F32)
            for c in range(slabs):
                buf[c, pl.ds(rho, tm // dil, stride=dil), :] = val[:, c * LANES:(c + 1) * LANES]
        return jnp.concatenate([buf[c] for c in range(slabs)], axis=1)

    l0 = natural(l0_ref, None)
    l1, l2 = natural(l1_ref, scr[0]), natural(l2_ref, scr[1])
    m = jnp.maximum(jnp.maximum(l0, l1), l2)
    w0, w1, w2 = jnp.exp(l0 - m), jnp.exp(l1 - m), jnp.exp(l2 - m)
    oc = (w0 * natural(oc0_ref, None) + w1 * natural(oc1_ref, scr[2])
          + w2 * natural(oc2_ref, scr[3])) * (1.0 / (w0 + w1 + w2))
    branches = (oa_ref[...], ob_ref[...], oc.astype(BF16), od_ref[...])
    merged = None
    for bi, ob in enumerate(branches):
        t = (jnp.dot(ob, wb_ref[bi], preferred_element_type=F32)
             * gate_ref[:, bi * D_MODEL:(bi + 1) * D_MODEL].astype(F32))
        merged = t if merged is None else merged + t
    o_ref[...] = x_ref[...] + jnp.dot(merged.astype(BF16), wo_ref[...], preferred_element_type=F32)


def _merge(x2, oa, ob, ocs, lses, od, gate, wb, wo, *, tm, seq):
    t, d = x2.shape
    per_seq = seq // tm
    row = lambda w: pl.BlockSpec((tm, w), lambda i: (i, 0))

    def classes(a):
        dil = a.shape[1]
        assert tm % (dil * 16) == 0
        return pl.BlockSpec((None, dil, tm // dil, BRANCH_W), lambda i: (i // per_seq, 0, i % per_seq, 0))

    return pl.pallas_call(
        _merge_kernel,
        out_shape=jax.ShapeDtypeStruct((t, d), F32),
        grid=(t // tm,),
        in_specs=([row(d), row(BRANCH_W), row(BRANCH_W)] + [classes(a) for a in ocs]
                  + [classes(a) for a in lses]
                  + [row(BRANCH_W), row(N_BRANCH * d),
                     pl.BlockSpec((N_BRANCH, BRANCH_W, d), lambda i: (0, 0, 0)),
                     pl.BlockSpec((d, d), lambda i: (0, 0))]),
        out_specs=row(d),
        scratch_shapes=[pltpu.VMEM((BRANCH_W // LANES, tm, LANES), F32)] * 4,
        compiler_params=_cparams(1),
        name="merge",
    )(x2, oa, ob, *ocs, *lses, od, gate, wb, wo)


def _ffn_kernel(x_ref, xp_ref, xn_ref, g2_ref, wa_ref, wg_ref, cwa_ref, cwg_ref, cba_ref, cbg_ref,
                wd_ref, o_ref, h_scr, acc_scr, *, tiles_per_seq):
    i, j = pl.program_id(0), pl.program_id(1)
    tm = x_ref.shape[0]
    ext = tm + 2 * CONV_HALO

    @pl.when(j == 0)
    def _():
        g2 = g2_ref[...]

        def norm(x):
            return x * lax.rsqrt(jnp.mean(x * x, axis=-1, keepdims=True) + EPS) * g2

        first = (i % tiles_per_seq) == 0
        last = (i % tiles_per_seq) == tiles_per_seq - 1
        h_scr[0:CONV_HALO, :] = jnp.where(first, 0.0, norm(xp_ref[...])).astype(BF16)
        h_scr[CONV_HALO:CONV_HALO + tm, :] = norm(x_ref[...]).astype(BF16)
        h_scr[CONV_HALO + tm:ext, :] = jnp.where(last, 0.0, norm(xn_ref[...])).astype(BF16)
        acc_scr[...] = jnp.zeros_like(acc_scr)

    h = h_scr[...]

    def conv(w_ref, cw_ref, cb_ref):
        u = jnp.dot(h, w_ref[...], preferred_element_type=F32)
        prev = pltpu.roll(u, 1, 0)[CONV_HALO:CONV_HALO + tm]
        nxt = pltpu.roll(u, ext - 1, 0)[CONV_HALO:CONV_HALO + tm]
        cw = cw_ref[...]
        return (cw[0:1] * prev + cw[1:2] * u[CONV_HALO:CONV_HALO + tm] + cw[2:3] * nxt
                + cb_ref[...])

    a = conv(wa_ref, cwa_ref, cba_ref)
    g = conv(wg_ref, cwg_ref, cbg_ref)
    hm = (a * (1.0 / (1.0 + jnp.exp(-a))) * g).astype(BF16)
    acc_scr[...] += jnp.dot(hm, wd_ref[...], preferred_element_type=F32)

    @pl.when(j == pl.num_programs(1) - 1)
    def _():
        o_ref[...] = x_ref[...] + acc_scr[...]


def _ffn(x2, g2, w_up, conv_w, conv_b, w_down, *, tm, tn, seq):
    t, d = x2.shape
    nj = D_FF // tn
    assert D_FF % tn == 0 and seq % tm == 0 and tm % CONV_HALO == 0
    hb = tm // CONV_HALO
    n_halo = t // CONV_HALO
    return pl.pallas_call(
        functools.partial(_ffn_kernel, tiles_per_seq=seq // tm),
        out_shape=jax.ShapeDtypeStruct((t, d), F32),
        grid=(t // tm, nj),
        in_specs=[pl.BlockSpec((tm, d), lambda i, j: (i, 0)),
                  pl.BlockSpec((CONV_HALO, d), lambda i, j: (jnp.maximum(i * hb - 1, 0), 0)),
                  pl.BlockSpec((CONV_HALO, d), lambda i, j: (jnp.minimum((i + 1) * hb, n_halo - 1), 0)),
                  pl.BlockSpec((1, d), lambda i, j: (0, 0)),
                  pl.BlockSpec((d, tn), lambda i, j: (0, j)),
                  pl.BlockSpec((d, tn), lambda i, j: (0, nj + j)),
                  pl.BlockSpec((3, tn), lambda i, j: (0, j)),
                  pl.BlockSpec((3, tn), lambda i, j: (0, nj + j)),
                  pl.BlockSpec((1, tn), lambda i, j: (0, j)),
                  pl.BlockSpec((1, tn), lambda i, j: (0, nj + j)),
                  pl.BlockSpec((tn, d), lambda i, j: (j, 0))],
        out_specs=pl.BlockSpec((tm, d), lambda i, j: (i, 0)),
        scratch_shapes=[pltpu.VMEM((tm + 2 * CONV_HALO, d), BF16), pltpu.VMEM((tm, d), F32)],
        compiler_params=_cparams(2),
        name="conv_ffn",
    )(x2, x2, x2, g2, w_up, w_up, conv_w, conv_w, conv_b, conv_b, w_down)


def _rope_tables(seq):
    pos = jnp.arange(seq, dtype=jnp.int32)

    def angles(p, dim):
        inv = ROPE_THETA ** (-(jnp.arange(0, dim, 2, dtype=F32) / dim))
        return p.astype(F32)[:, None] * inv[None, :]

    def tables(ang_per_lane, first_half):
        c, s = jnp.cos(ang_per_lane), jnp.sin(ang_per_lane)
        sa = jnp.where(first_half[None, :], -s, 0.0)
        sb = jnp.where(first_half[None, :], 0.0, s)
        return tuple(jnp.tile(t, (1, LANES // HEAD_DIM)) for t in (c, sa, sb))

    lane = np.arange(HEAD_DIM)
    half = HEAD_DIM // 2
    a1 = angles(pos, HEAD_DIM)
    t1 = tables(jnp.concatenate([a1, a1], axis=1), lane < half)
    ar = angles(pos // GRID_W, half)
    ac = angles(pos % GRID_W, half)
    t2 = tables(jnp.concatenate([ar, ar, ac, ac], axis=1), (lane % half) < half // 2)
    return t1, t2


def _cols(w, name, part=None):
    o, n = _OFF[name]
    if part is not None:
        n //= len(DIL_GROUPS)
        o += part * n
    return w[:, o:o + n]


def _dup_heads(w):
    k0, k1 = w[:, :HEAD_DIM], w[:, HEAD_DIM:]
    return jnp.concatenate([k0, k0, k1, k1], axis=1)


def _head_gain(g, n_heads):
    return jnp.tile(g.astype(F32), n_heads)


def _by_residue(table, tm, dil):
    s = table.shape[0]
    return table.reshape(s // tm, tm // dil, dil, -1).transpose(0, 2, 1, 3).reshape(s, -1)


def kernel(x, norm1_g, w_in, qk_g, lam, subln_g, rpb, w_branch, w_out, norm2_g, w_up, conv_w, conv_b, w_down):
    b, s, d = x.shape
    depth = w_in.shape[0]
    t = b * s
    rows = s // GRID_W
    tm = min(TOKEN_TILE, s)
    t1d, tax = _rope_tables(s)
    t1d_dil = [tuple(_by_residue(tb, tm, dil) for tb in t1d) for _, dil in DIL_GROUPS]
    x2 = x.reshape(t, d)

    for l in range(depth):
        lambda_init = 0.8 - 0.6 * math.exp(-0.3 * l)
        w = w_in[l]
        w_r1 = jnp.concatenate([_cols(w, "aq"), _cols(w, "ak")], axis=1).astype(BF16)
        w_c = [jnp.concatenate([_cols(w, n, gi) for n in ("cq", "ck", "cv")], axis=1).astype(BF16)
               for gi in range(len(DIL_GROUPS))]
        w_r2 = jnp.concatenate([_cols(w, "bq"), _dup_heads(_cols(w, "bk"))], axis=1).astype(BF16)
        w_r3 = jnp.concatenate([_cols(w, "dq"), _cols(w, "dk")], axis=1).astype(BF16)
        w_pv = jnp.concatenate([_cols(w, "dv"), _cols(w, "av"), _dup_heads(_cols(w, "bv"))],
                               axis=1).astype(BF16)
        w_g = _cols(w, "g").astype(BF16)
        gq = qk_g[l]
        gain1 = jnp.concatenate([_head_gain(gq[0, 0], 8), _head_gain(gq[0, 1], 8)])[None, :]
        gain_c = jnp.concatenate([_head_gain(gq[2, 0], 8), _head_gain(gq[2, 1], 8),
                                  jnp.ones((BRANCH_W,), F32)])[None, :]
        gain2 = jnp.concatenate([_head_gain(gq[1, 0], 8), _head_gain(gq[1, 1], 4)])[None, :]
        gain3 = jnp.concatenate([_head_gain(gq[3, 0], 8), _head_gain(gq[3, 1], 8)])[None, :]
        g1 = norm1_g[l].astype(F32)[None, :]

        half = HEAD_DIM // 2
        r1 = _proj(x2, g1, w_r1, mode="rope", tm=tm, tn=1024, seq=s, gain=gain1, tables=t1d,
                   shift=half).reshape(b, s, -1)
        qkv_c = [_proj(x2, g1, w_c[gi], mode="rope", tm=tm, tn=3 * BRANCH_W, seq=s, gain=gain_c,
                       tables=t1d_dil[gi], shift=half, dil=dil, rope_cols=2 * BRANCH_W)
                 for gi, (_, dil) in enumerate(DIL_GROUPS)]
        r2 = _proj(x2, g1, w_r2, mode="rope", tm=tm, tn=768, seq=s, gain=gain2, tables=tax,
                   shift=half // 2).reshape(b, s, -1)
        r3 = _proj(x2, g1, w_r3, mode="norm", tm=tm, tn=1024, seq=s, gain=gain3).reshape(b, s, -1)
        pv = _proj(x2, g1, w_pv, mode="plain", tm=tm, tn=1280, seq=s).reshape(b, s, -1)
        gate = _proj(x2, g1, w_g, mode="sigmoid", tm=tm, tn=1024, seq=s)

        o_a = _diff_attention(r1, pv, lam[l].astype(F32), subln_g[l].astype(F32)[None, :],
                              lambda_init, tq=min(ATTN_Q_TILE, s))
        o_b = _gqa_attention(r2, pv, tq=min(ATTN_Q_TILE, s))
        ocs, lses = zip(*[_dilated_group(qkv) for qkv in qkv_c])
        o_d = _nbr_attention(r3, pv, _nbr_bias_table(rpb[l], rows))

        x2 = _merge(x2, o_a.reshape(t, BRANCH_W), o_b.reshape(t, BRANCH_W), ocs, lses,
                    o_d.reshape(t, BRANCH_W), gate, w_branch[l].astype(BF16), w_out[l].astype(BF16),
                    tm=min(MERGE_TILE, s), seq=s)
        x2 = _ffn(x2, norm2_g[l].astype(F32)[None, :], w_up[l].astype(BF16), conv_w[l].astype(F32),
                  conv_b[l].astype(F32)[None, :], w_down[l].astype(BF16), tm=tm, tn=FFN_COL_TILE, seq=s)
    return x2.reshape(b, s, d)
```
